```python
import math
import jax
import jax.numpy as jnp
from jax import lax
import numpy as np

D_MODEL = 1024
BATCH = 4
SEQ = 4096
DEPTH = 2
DEC_BATCH = 128
DEC_SEQ = 1
PAST_LEN = 2048
PAGE_SIZE = 128

N_EVEN = (DEPTH + 1) // 2
N_ODD = DEPTH // 2
D_MIX = D_MODEL
H_A = 4
DV_A = D_MIX // 2 // H_A
DK_A = DV_A // 2
GLA_RANK = 16
GLA_TAU = 16.0
H_B = 4
DK_B = D_MIX // 2 // H_B
DV_B = DK_B
CONV_W = 4
H_C = 8
HD_C = D_MIX // 2 // H_C
D_RNN = D_MIX // 2
H_D = 8
BD_D = D_RNN // H_D
RG_C = 8.0
D_FF = ((8 * D_MODEL // 3 + 255) // 256) * 256
CHUNK = 64
Q_BLOCK = 128
EPS = 1e-6
EVEN_SIZES = (H_A * DK_A, H_A * DK_A, H_A * DV_A, H_A * DV_A, GLA_RANK, 3 * H_B * DK_B, H_B * DV_B, H_B, H_B)
ODD_SIZES = (H_C * HD_C, H_C * HD_C, H_C * HD_C, D_RNN, D_RNN)
D_IN_EVEN = sum(EVEN_SIZES)
D_IN_ODD = sum(ODD_SIZES)

kernel_name = 'hybrid_gla_gdn_stickbreak_rglru_macaron_step'


def split_cols(x, sizes):
    out, start = [], 0
    for s in sizes:
        out.append(x[..., start:start + s])
        start += s
    return out


def rmsnorm(x, g):
    xf = x.astype(jnp.float32)
    y = xf * lax.rsqrt(jnp.mean(xf * xf, axis=-1, keepdims=True) + EPS)
    return (y * g.astype(jnp.float32)).astype(x.dtype)


def l2norm(x):
    return x * lax.rsqrt(jnp.sum(x * x, axis=-1, keepdims=True) + EPS)


def half_ffn(x, g, w_gate, w_up, w_down):
    h = rmsnorm(x, g)
    return 0.5 * ((jax.nn.silu(h @ w_gate) * (h @ w_up)) @ w_down)


def causal_dwconv(x, hist, w):
    xe = jnp.concatenate([hist.astype(x.dtype), x], axis=1)
    y = lax.conv_general_dilated(xe, w[:, None, :].astype(x.dtype), (1,), 'VALID',
                                 dimension_numbers=('NWC', 'WIO', 'NWC'), feature_group_count=x.shape[-1])
    return y, xe[:, xe.shape[1] - (CONV_W - 1):]


def to_chunks(a, c):
    b, t = a.shape[:2]
    n = -(-t // c)
    a = jnp.pad(a, [(0, 0), (0, n * c - t)] + [(0, 0)] * (a.ndim - 2))
    return jnp.moveaxis(a.reshape((b, n, c) + a.shape[2:]), 1, 0)


def from_chunks(a, t):
    a = jnp.moveaxis(a, 0, 1)
    return a.reshape((a.shape[0], -1) + a.shape[3:])[:, :t]


def gla_chunked(q, k, v, log_g, s0):
    t = q.shape[1]
    c = min(CHUNK, t)
    incl = jnp.tril(jnp.ones((c, c), bool))

    def step(s, inp):
        qi, ki, vi, gi = inp
        g = jnp.cumsum(gi, axis=1)
        rel = jnp.exp(jnp.where(incl[None, :, :, None, None], g[:, :, None] - g[:, None, :], -jnp.inf))
        att = jnp.einsum('bthd,btshd,bshd->bhts', qi, rel, ki)
        o = jnp.einsum('bhts,bshv->bthv', att, vi) + jnp.einsum('bthd,bhdv->bthv', qi * jnp.exp(g), s)
        g_end = g[:, -1]
        s = s * jnp.exp(g_end)[..., None] + jnp.einsum('bshd,bshv->bhdv', ki * jnp.exp(g_end[:, None] - g), vi)
        return s, o

    s, o = lax.scan(step, s0, (to_chunks(q, c), to_chunks(k, c), to_chunks(v, c), to_chunks(log_g, c)))
    return from_chunks(o, t), s


def gdn_chunked(q, k, v, log_g, beta, s0):
    t = q.shape[1]
    c = min(CHUNK, t)
    dk = q.shape[-1]
    incl = jnp.tril(jnp.ones((c, c), bool))
    strict = jnp.tril(jnp.ones((c, c), bool), -1)
    eye = jnp.eye(c, dtype=jnp.float32)

    def step(s, inp):
        qi, ki, vi, gi, bi = inp
        g = jnp.cumsum(gi, axis=1).swapaxes(1, 2)
        bh = bi.swapaxes(1, 2)
        rel = jnp.exp(jnp.where(incl, g[..., :, None] - g[..., None, :], -jnp.inf))
        kk = jnp.einsum('bthd,bshd->bhts', ki, ki)
        m = eye + jnp.where(strict, kk * rel, 0.0) * bh[..., :, None]
        kh = ki.swapaxes(1, 2)
        vh = vi.swapaxes(1, 2)
        rhs = jnp.concatenate([kh * (bh * jnp.exp(g))[..., None], vh * bh[..., None]], axis=-1)
        sol = lax.linalg.triangular_solve(m, rhs, left_side=True, lower=True, unit_diagonal=True)
        w, u = sol[..., :dk], sol[..., dk:]
        v_new = u - jnp.einsum('bhtd,bhdv->bhtv', w, s)
        qk = jnp.einsum('bthd,bshd->bhts', qi, ki) * rel
        o = jnp.einsum('bhts,bhsv->bhtv', qk, v_new) + jnp.einsum('bthd,bht,bhdv->bhtv', qi, jnp.exp(g), s)
        g_end = g[..., -1]
        s = s * jnp.exp(g_end)[..., None, None] + jnp.einsum('bhsd,bhs,bhsv->bhdv', kh, jnp.exp(g_end[..., None] - g), v_new)
        return s, o.swapaxes(1, 2)

    s, o = lax.scan(step, s0, (to_chunks(q, c), to_chunks(k, c), to_chunks(v, c), to_chunks(log_g, c), to_chunks(beta, c)))
    return from_chunks(o, t), s


def stick_breaking(q, k, v, bias, offset):
    bsz, tq, h, d = q.shape
    tk = k.shape[1]
    qb = min(Q_BLOCK, tq)
    nb = -(-tq // qb)
    qp = jnp.moveaxis(jnp.pad(q, ((0, 0), (0, nb * qb - tq), (0, 0), (0, 0))).reshape(bsz, nb, qb, h, d), 1, 0)
    key_idx = jnp.arange(tk)
    scale = d ** -0.5
    bias = bias.astype(jnp.float32)[None, :, None, None]

    def block(args):
        qblk, b = args
        qpos = offset + b * qb + jnp.arange(qb)
        z = jnp.einsum('bqhd,bkhd->bhqk', qblk, k) * scale + bias
        mask = key_idx[None, :] < qpos[:, None]
        log_1mb = jnp.where(mask, jax.nn.log_sigmoid(-z), 0.0)
        acc = lax.cumsum(log_1mb, axis=3, reverse=True) - log_1mb
        wts = jnp.where(mask, jnp.exp(jax.nn.log_sigmoid(z) + acc), 0.0)
        return jnp.einsum('bhqk,bkhd->bqhd', wts, v)

    o = lax.map(block, (qp, jnp.arange(nb)))
    return from_chunks(o, tq)


def rg_lru(x, h0, offset, w_a, b_a, w_x, b_x, lam):
    bsz, t, _ = x.shape
    xb = x.reshape(bsz, t, H_D, BD_D)
    r = jax.nn.sigmoid(jnp.einsum('bthi,hij->bthj', xb, w_a).reshape(bsz, t, D_RNN) + b_a)
    i = jax.nn.sigmoid(jnp.einsum('bthi,hij->bthj', xb, w_x).reshape(bsz, t, D_RNN) + b_x)
    log_a = -RG_C * r * jax.nn.softplus(-lam)
    reset = ((offset + jnp.arange(t)) == 0)[None, :, None]
    a = jnp.where(reset, 0.0, jnp.exp(log_a))
    mult = jnp.where(reset, 1.0, jnp.sqrt(-jnp.expm1(2.0 * log_a)))
    bterm = (mult * (i * x)).at[:, 0].add(a[:, 0] * h0)
    _, hs = lax.associative_scan(lambda l, rr: (l[0] * rr[0], rr[0] * l[1] + rr[1]), (a, bterm), axis=1)
    return hs


def even_mixer(h, s_gla, s_gdn, conv_hist, w_in, gla_w_gate, gla_b_gate, gla_onorm,
               gdn_conv_w, gdn_a_log, gdn_dt_bias, gdn_onorm, w_out):
    bsz, t, _ = h.shape
    f32 = jnp.float32
    qa, ka, va, ra, la, qkv_b, zb, ab, bb = split_cols(h @ w_in, EVEN_SIZES)
    qa = qa.astype(f32).reshape(bsz, t, H_A, DK_A) * DK_A ** -0.5
    ka = ka.astype(f32).reshape(bsz, t, H_A, DK_A)
    va = va.astype(f32).reshape(bsz, t, H_A, DV_A)
    log_g = (jax.nn.log_sigmoid((la @ gla_w_gate).astype(f32) + gla_b_gate) / GLA_TAU).reshape(bsz, t, H_A, DK_A)
    oa, s_gla = gla_chunked(qa, ka, va, log_g, s_gla.astype(f32))
    oa = rmsnorm(oa, gla_onorm) * jax.nn.silu(ra.astype(f32).reshape(bsz, t, H_A, DV_A))
    qkv, conv_hist = causal_dwconv(qkv_b, conv_hist, gdn_conv_w)
    qb, kb, vb = split_cols(jax.nn.silu(qkv.astype(f32)), (H_B * DK_B, H_B * DK_B, H_B * DV_B))
    qb = l2norm(qb.reshape(bsz, t, H_B, DK_B)) * DK_B ** -0.5
    kb = l2norm(kb.reshape(bsz, t, H_B, DK_B))
    vb = vb.reshape(bsz, t, H_B, DV_B)
    log_gb = -jnp.exp(gdn_a_log.astype(f32)) * jax.nn.softplus(ab.astype(f32) + gdn_dt_bias)
    beta = jax.nn.sigmoid(bb.astype(f32))
    ob, s_gdn = gdn_chunked(qb, kb, vb, log_gb, beta, s_gdn.astype(f32))
    ob = rmsnorm(ob, gdn_onorm) * jax.nn.silu(zb.astype(f32).reshape(bsz, t, H_B, DV_B))
    o = jnp.concatenate([oa.reshape(bsz, t, -1), ob.reshape(bsz, t, -1)], axis=-1).astype(h.dtype)
    return o @ w_out, s_gla, s_gdn, conv_hist


def odd_mixer(h, k_past, v_past, h0, conv_hist, w_in, sb_bias, rg_conv_w, rg_conv_b, rg_w_a, rg_b_a,
              rg_w_x, rg_b_x, rg_lambda, w_out):
    bsz, t, _ = h.shape
    f32 = jnp.float32
    qc, kc, vc, xr, yr = split_cols(h @ w_in, ODD_SIZES)
    qc = qc.reshape(bsz, t, H_C, HD_C)
    kc = kc.reshape(bsz, t, H_C, HD_C)
    vc = vc.reshape(bsz, t, H_C, HD_C)
    offset = k_past.shape[1]
    k_all = jnp.concatenate([k_past.astype(f32), kc.astype(f32)], axis=1)
    v_all = jnp.concatenate([v_past.astype(f32), vc.astype(f32)], axis=1)
    oc = stick_breaking(qc.astype(f32), k_all, v_all, sb_bias, offset)
    xc, conv_hist = causal_dwconv(xr, conv_hist, rg_conv_w)
    hs = rg_lru(xc.astype(f32) + rg_conv_b, h0.astype(f32), offset, rg_w_a, rg_b_a, rg_w_x, rg_b_x, rg_lambda)
    od = hs * jax.nn.gelu(yr.astype(f32))
    o = jnp.concatenate([oc.reshape(bsz, t, -1), od], axis=-1).astype(h.dtype)
    return o @ w_out, kc, vc, hs[:, -1], conv_hist


def setup_inputs(seed: int = 0) -> dict:
    key = jax.random.key(seed)
    ks = jax.random.split(key, 40)
    f32 = jnp.float32

    def nrm(i, shape, scale=1.0):
        return scale * jax.random.normal(ks[i], shape, f32)

    n_pages = PAST_LEN // PAGE_SIZE
    n_used = DEC_BATCH * n_pages
    n_pool = n_used + (n_used + 3) // 4
    page_table = jax.random.permutation(ks[9], n_pool)[:n_used].reshape(DEC_BATCH, n_pages).astype(jnp.int32)
    dt = jnp.exp(jax.random.uniform(ks[20], (N_EVEN, H_B), f32, minval=math.log(1e-3), maxval=math.log(1e-1)))
    a_c = jax.random.uniform(ks[32], (N_ODD, D_RNN), f32, minval=0.9, maxval=0.999)
    sig = a_c ** (1.0 / RG_C)
    return {
        'x_prompt': nrm(0, (BATCH, SEQ, D_MODEL)),
        'x_sample': nrm(1, (DEC_BATCH, DEC_SEQ, D_MODEL)),
        'state_gla': nrm(2, (N_EVEN, DEC_BATCH, H_A, DK_A, DV_A)),
        'state_gdn': nrm(3, (N_EVEN, DEC_BATCH, H_B, DK_B, DV_B), 0.5),
        'state_gdn_conv': nrm(4, (N_EVEN, DEC_BATCH, CONV_W - 1, 3 * H_B * DK_B)),
        'cache_sb_k': nrm(5, (N_ODD, n_pool, PAGE_SIZE, H_C, HD_C)),
        'cache_sb_v': nrm(6, (N_ODD, n_pool, PAGE_SIZE, H_C, HD_C)),
        'state_rg_h': nrm(7, (N_ODD, DEC_BATCH, D_RNN), 0.5),
        'state_rg_conv': nrm(8, (N_ODD, DEC_BATCH, CONV_W - 1, D_RNN)),
        'page_table': page_table,
        'ln_ffn': 1.0 + nrm(10, (DEPTH, 2, D_MODEL), 0.02),
        'ffn_w_gate': nrm(11, (DEPTH, 2, D_MODEL, D_FF), D_MODEL ** -0.5),
        'ffn_w_up': nrm(12, (DEPTH, 2, D_MODEL, D_FF), D_MODEL ** -0.5),
        'ffn_w_down': nrm(13, (DEPTH, 2, D_FF, D_MODEL), D_FF ** -0.5),
        'ln_mix_e': 1.0 + nrm(14, (N_EVEN, D_MODEL), 0.02),
        'w_in_e': nrm(15, (N_EVEN, D_MODEL, D_IN_EVEN), D_MODEL ** -0.5),
        'gla_w_gate': nrm(16, (N_EVEN, GLA_RANK, H_A * DK_A), GLA_RANK ** -0.5),
        'gla_b_gate': nrm(17, (N_EVEN, H_A * DK_A), 0.01),
        'gla_onorm': 1.0 + nrm(18, (N_EVEN, DV_A), 0.02),
        'gdn_conv_w': nrm(19, (N_EVEN, CONV_W, 3 * H_B * DK_B), CONV_W ** -0.5),
        'gdn_a_log': jnp.log(jax.random.uniform(ks[21], (N_EVEN, H_B), f32, minval=1.0, maxval=16.0)),
        'gdn_dt_bias': dt + jnp.log(-jnp.expm1(-dt)),
        'gdn_onorm': 1.0 + nrm(22, (N_EVEN, DV_B), 0.02),
        'w_out_e': nrm(23, (N_EVEN, D_MIX, D_MODEL), D_MIX ** -0.5),
        'ln_mix_o': 1.0 + nrm(24, (N_ODD, D_MODEL), 0.02),
        'w_in_o': nrm(25, (N_ODD, D_MODEL, D_IN_ODD), D_MODEL ** -0.5),
        'sb_bias': jax.random.uniform(ks[35], (N_ODD, H_C), f32, minval=-7.5, maxval=-5.5),
        'rg_conv_w': nrm(26, (N_ODD, CONV_W, D_RNN), CONV_W ** -0.5),
        'rg_conv_b': nrm(27, (N_ODD, D_RNN), 0.01),
        'rg_w_a': nrm(28, (N_ODD, H_D, BD_D, BD_D), BD_D ** -0.5),
        'rg_b_a': nrm(29, (N_ODD, D_RNN), 0.01),
        'rg_w_x': nrm(30, (N_ODD, H_D, BD_D, BD_D), BD_D ** -0.5),
        'rg_b_x': nrm(31, (N_ODD, D_RNN), 0.01),
        'rg_lambda': jnp.log(sig) - jnp.log1p(-sig),
        'w_out_o': nrm(33, (N_ODD, D_MIX, D_MODEL), D_MIX ** -0.5),
        'ln_final': 1.0 + nrm(34, (D_MODEL,), 0.02),
    }


def reference(x_prompt, x_sample, state_gla, state_gdn, state_gdn_conv, cache_sb_k, cache_sb_v,
              state_rg_h, state_rg_conv, page_table, ln_ffn, ffn_w_gate, ffn_w_up, ffn_w_down,
              ln_mix_e, w_in_e, gla_w_gate, gla_b_gate, gla_onorm, gdn_conv_w, gdn_a_log, gdn_dt_bias,
              gdn_onorm, w_out_e, ln_mix_o, w_in_o, sb_bias, rg_conv_w, rg_conv_b, rg_w_a, rg_b_a, rg_w_x, rg_b_x,
              rg_lambda, w_out_o, ln_final):
    f32 = jnp.float32
    xp, xs = x_prompt, x_sample
    bp, bs = xp.shape[0], xs.shape[0]
    n_pages = page_table.shape[1]
    page = cache_sb_k.shape[2]
    gla_p, gla_s, gdn_p, gdn_s, gcv_p, gcv_s = [], [], [], [], [], []
    sbk_p, sbk_s, sbv_p, sbv_s, rgh_p, rgh_s, rgc_p, rgc_s = [], [], [], [], [], [], [], []
    for layer in range(DEPTH):
        xp = xp + half_ffn(xp, ln_ffn[layer, 0], ffn_w_gate[layer, 0], ffn_w_up[layer, 0], ffn_w_down[layer, 0])
        xs = xs + half_ffn(xs, ln_ffn[layer, 0], ffn_w_gate[layer, 0], ffn_w_up[layer, 0], ffn_w_down[layer, 0])
        if layer % 2 == 0:
            e = layer // 2
            wts = (w_in_e[e], gla_w_gate[e], gla_b_gate[e], gla_onorm[e], gdn_conv_w[e], gdn_a_log[e],
                   gdn_dt_bias[e], gdn_onorm[e], w_out_e[e])
            mp, s1, s2, s3 = even_mixer(rmsnorm(xp, ln_mix_e[e]), jnp.zeros((bp, H_A, DK_A, DV_A), f32),
                                        jnp.zeros((bp, H_B, DK_B, DV_B), f32),
                                        jnp.zeros((bp, CONV_W - 1, 3 * H_B * DK_B), xp.dtype), *wts)
            ms, t1, t2, t3 = even_mixer(rmsnorm(xs, ln_mix_e[e]), state_gla[e], state_gdn[e], state_gdn_conv[e], *wts)
            gla_p.append(s1); gdn_p.append(s2); gcv_p.append(s3)
            gla_s.append(t1); gdn_s.append(t2); gcv_s.append(t3)
        else:
            o = layer // 2
            wts = (w_in_o[o], sb_bias[o], rg_conv_w[o], rg_conv_b[o], rg_w_a[o], rg_b_a[o], rg_w_x[o], rg_b_x[o],
                   rg_lambda[o], w_out_o[o])
            empty = jnp.zeros((bp, 0, H_C, HD_C), xp.dtype)
            mp, s1, s2, s3, s4 = odd_mixer(rmsnorm(xp, ln_mix_o[o]), empty, empty, jnp.zeros((bp, D_RNN), f32),
                                           jnp.zeros((bp, CONV_W - 1, D_RNN), xp.dtype), *wts)
            k_past = cache_sb_k[o][page_table].reshape(bs, n_pages * page, H_C, HD_C)
            v_past = cache_sb_v[o][page_table].reshape(bs, n_pages * page, H_C, HD_C)
            ms, t1, t2, t3, t4 = odd_mixer(rmsnorm(xs, ln_mix_o[o]), k_past, v_past, state_rg_h[o], state_rg_conv[o], *wts)
            sbk_p.append(s1); sbv_p.append(s2); rgh_p.append(s3); rgc_p.append(s4)
            sbk_s.append(t1); sbv_s.append(t2); rgh_s.append(t3); rgc_s.append(t4)
        xp = xp + mp
        xs = xs + ms
        xp = xp + half_ffn(xp, ln_ffn[layer, 1], ffn_w_gate[layer, 1], ffn_w_up[layer, 1], ffn_w_down[layer, 1])
        xs = xs + half_ffn(xs, ln_ffn[layer, 1], ffn_w_gate[layer, 1], ffn_w_up[layer, 1], ffn_w_down[layer, 1])
    y_prompt = rmsnorm(xp, ln_final)
    y_sample = rmsnorm(xs, ln_final)
    new_gla_p = jnp.stack(gla_p).astype(state_gla.dtype)
    new_gla_s = jnp.stack(gla_s).astype(state_gla.dtype)
    new_gdn_p = jnp.stack(gdn_p).astype(state_gdn.dtype)
    new_gdn_s = jnp.stack(gdn_s).astype(state_gdn.dtype)
    new_gcv_p = jnp.stack(gcv_p).astype(state_gdn_conv.dtype)
    new_gcv_s = jnp.stack(gcv_s).astype(state_gdn_conv.dtype)
    new_sbk_p = jnp.stack(sbk_p).astype(cache_sb_k.dtype)
    new_sbk_s = jnp.stack(sbk_s).astype(cache_sb_k.dtype)
    new_sbv_p = jnp.stack(sbv_p).astype(cache_sb_v.dtype)
    new_sbv_s = jnp.stack(sbv_s).astype(cache_sb_v.dtype)
    new_rgh_p = jnp.stack(rgh_p).astype(state_rg_h.dtype)
    new_rgh_s = jnp.stack(rgh_s).astype(state_rg_h.dtype)
    new_rgc_p = jnp.stack(rgc_p).astype(state_rg_conv.dtype)
    new_rgc_s = jnp.stack(rgc_s).astype(state_rg_conv.dtype)
    return (y_prompt, y_sample, new_gla_p, new_gla_s, new_gdn_p, new_gdn_s, new_gcv_p, new_gcv_s,
            new_sbk_p, new_sbk_s, new_sbv_p, new_sbv_s, new_rgh_p, new_rgh_s, new_rgc_p, new_rgc_s)
```

```python
import functools
import math

import jax
import jax.numpy as jnp
from jax import lax
from jax.experimental import pallas as pl
from jax.experimental.pallas import tpu as pltpu

F32 = jnp.float32
BF16 = jnp.bfloat16

H_A = 4
GLA_TAU = 16.0
H_B = 4
CONV_W = 4
H_C = 8
H_D = 8
RG_C = 8.0
CHUNK = 64
EPS = 1e-6

VMEM_LIMIT_BYTES = 56 * 1024 * 1024
LANES = 128


def _cparams(*sem):
    return pltpu.CompilerParams(dimension_semantics=sem, vmem_limit_bytes=VMEM_LIMIT_BYTES)


def _dot(a, b):
    return jnp.dot(a.astype(BF16), b.astype(BF16), preferred_element_type=F32)


def _dot_nt(a, b):
    return lax.dot_general(a.astype(BF16), b.astype(BF16), (((1,), (1,)), ((), ())), preferred_element_type=F32)


def _dot_tn(a, b):
    return lax.dot_general(a.astype(BF16), b.astype(BF16), (((0,), (0,)), ((), ())), preferred_element_type=F32)


def _split2(x):
    hi = x.astype(BF16)
    lo = (x - hi.astype(F32)).astype(BF16)
    return hi, lo


def _split3(x):
    x1 = x.astype(BF16)
    r1 = x - x1.astype(F32)
    x2 = r1.astype(BF16)
    x3 = (r1 - x2.astype(F32)).astype(BF16)
    return x1, x2, x3


def _dot_hi(a, b):
    a1, a2 = _split2(a)
    b1, b2 = _split2(b)
    d = lambda u, v: jnp.dot(u, v, preferred_element_type=F32)
    return d(a1, b1) + (d(a1, b2) + d(a2, b1))


def _mask_dot(m01, x):
    x1, x2, x3 = _split3(x)
    d = lambda v: jnp.dot(m01, v, preferred_element_type=F32)
    return d(x1) + (d(x2) + d(x3))


def _rms(x, g):
    return x * lax.rsqrt(jnp.mean(x * x, axis=-1, keepdims=True) + EPS) * g


def _sigmoid(x):
    return 1.0 / (1.0 + jnp.exp(-x))


def _silu(x):
    return x * _sigmoid(x)


def _softplus(x):
    return jnp.maximum(x, 0.0) + jnp.log1p(jnp.exp(-jnp.abs(x)))


def _gelu_tanh(x):
    return 0.5 * x * (1.0 + jnp.tanh(math.sqrt(2.0 / math.pi) * (x + 0.044715 * (x * x * x))))


def _one_minus_exp2(log_a, a):
    return jnp.tanh(-log_a) * (a * a + 1.0)


def _const_spec(shape):
    nd = len(shape)
    return pl.BlockSpec(shape, lambda *_: (0,) * nd, pipeline_mode=pl.Buffered(1))


def _block_kernel(*refs, n_mix, n_proj, has_gate, has_final, ff_chunk):
    it = iter(refs)
    x_ref = next(it)
    o_refs = [next(it) for _ in range(n_mix)]
    wo_refs = [next(it) for _ in range(n_mix)]
    gf_ref, wg_ref, wu_ref, wd_ref = next(it), next(it), next(it), next(it)
    if has_final:
        gfin_ref = next(it)
    if n_proj:
        gm_ref = next(it)
        wp_refs = [next(it) for _ in range(n_proj)]
    if has_gate:
        wgate_ref, bgate_ref = next(it), next(it)
    xo_ref = next(it)
    po_refs = [next(it) for _ in range(n_proj)]

    x = x_ref[...]
    for o_ref, wo_ref in zip(o_refs, wo_refs):
        x = x + _dot(o_ref[...], wo_ref[...])
    h = _rms(x, gf_ref[...]).astype(BF16)
    d_ff = wg_ref.shape[1]
    acc = jnp.zeros_like(x)
    for c0 in range(0, d_ff, ff_chunk):
        gt = jnp.dot(h, wg_ref[:, c0:c0 + ff_chunk], preferred_element_type=F32)
        up = jnp.dot(h, wu_ref[:, c0:c0 + ff_chunk], preferred_element_type=F32)
        act = (_silu(gt) * up).astype(BF16)
        acc = acc + jnp.dot(act, wd_ref[c0:c0 + ff_chunk, :], preferred_element_type=F32)
    x = x + 0.5 * acc
    if has_final:
        xo_ref[...] = _rms(x, gfin_ref[...])
    else:
        xo_ref[...] = x
    if n_proj:
        hm = _rms(x, gm_ref[...]).astype(BF16)
        for j, (wp_ref, po_ref) in enumerate(zip(wp_refs, po_refs)):
            p = jnp.dot(hm, wp_ref[...], preferred_element_type=F32)
            if has_gate and j == n_proj - 1:
                p = _dot(p, wgate_ref[...]) + bgate_ref[...]
            po_ref[...] = p


def _block_call(x, mix, ffn, final_g=None, proj=None, gate=None, tm=256):
    m, d = x.shape
    tm = min(tm, m)
    assert m % tm == 0
    g_ffn, wg, wu, wd = ffn
    d_ff = wg.shape[1]
    ff_chunk = d_ff // 2 if (d_ff // 2) % LANES == 0 else d_ff
    row = lambda n: pl.BlockSpec((tm, n), lambda i: (i, 0))
    args, specs = [x], [row(d)]
    o_list, wo_list = mix if mix is not None else ((), ())
    for o in o_list:
        args.append(o)
        specs.append(row(o.shape[1]))
    for w in wo_list:
        args.append(w)
        specs.append(_const_spec(w.shape))
    for a in (g_ffn, wg, wu, wd):
        args.append(a)
        specs.append(_const_spec(a.shape))
    if final_g is not None:
        args.append(final_g)
        specs.append(_const_spec(final_g.shape))
    out_shapes, out_specs = [jax.ShapeDtypeStruct((m, d), F32)], [row(d)]
    n_proj = 0
    if proj is not None:
        g_mix, w_list = proj
        n_proj = len(w_list)
        args.append(g_mix)
        specs.append(_const_spec(g_mix.shape))
        for w in w_list:
            args.append(w)
            specs.append(_const_spec(w.shape))
        widths = [w.shape[1] for w in w_list]
        if gate is not None:
            for a in gate:
                args.append(a)
                specs.append(_const_spec(a.shape))
            widths[-1] = gate[0].shape[1]
        for n in widths:
            out_shapes.append(jax.ShapeDtypeStruct((m, n), F32))
            out_specs.append(row(n))
    kern = functools.partial(_block_kernel, n_mix=len(o_list), n_proj=n_proj, has_gate=gate is not None,
                             has_final=final_g is not None, ff_chunk=ff_chunk)
    return pl.pallas_call(
        kern, grid=(m // tm,), in_specs=specs, out_specs=out_specs, out_shape=out_shapes,
        compiler_params=_cparams("parallel"), name="block")(*args)


def _tri_incl(c):
    r = lax.broadcasted_iota(jnp.int32, (c, c), 0)
    s = lax.broadcasted_iota(jnp.int32, (c, c), 1)
    return r >= s


def _gla_kernel(q_ref, k_ref, v_ref, r_ref, lg_ref, on_ref, o_ref, st_ref, s_scr, *, bb, dk, dv):
    c_idx = pl.program_id(1)
    c = q_ref.shape[1]
    n_pairs = q_ref.shape[2] // LANES
    hpp = LANES // dk

    @pl.when(c_idx == 0)
    def _():
        s_scr[...] = jnp.zeros_like(s_scr)

    incl = _tri_incl(c)
    tri01 = jnp.where(incl, 1.0, 0.0).astype(BF16)
    lane_head = lax.broadcasted_iota(jnp.int32, (c, LANES), 1) // dk
    onorm = on_ref[...]
    for b in range(bb):
        lg = -_softplus(-lg_ref[b]) * (1.0 / GLA_TAU)
        g = _mask_dot(tri01, lg)
        g_mid = g[c // 2 - 1:c // 2, :]
        g_end = g[c - 1:c, :]
        qs = q_ref[b] * (dk ** -0.5)
        kk = k_ref[b]
        q_state = qs * jnp.exp(g)
        q_in = qs * jnp.exp(g - g_mid)
        k_in = kk * jnp.exp(g_mid - g)
        k_end = kk * jnp.exp(g_end - g)
        for p in range(n_pairs):
            sl = slice(p * LANES, (p + 1) * LANES)
            st = s_scr[b, p]
            st_bf = st.astype(BF16)
            upd = None
            for hh in range(hpp):
                h = p * hpp + hh
                vs = slice(h * dv, (h + 1) * dv)
                sel = lane_head == hh
                att = _dot_nt(jnp.where(sel, q_in[:, sl], 0.0), k_in[:, sl])
                att = jnp.where(incl, att, 0.0)
                vh = v_ref[b, :, vs]
                o = _dot(att, vh) + _dot_nt(jnp.where(sel, q_state[:, sl], 0.0), st_bf)
                o = _rms(o, onorm) * _silu(r_ref[b, :, vs])
                o_ref[b, :, vs] = o.astype(o_ref.dtype)
                u = _dot_tn(vh, jnp.where(sel, k_end[:, sl], 0.0))
                upd = u if upd is None else upd + u
            s_scr[b, p] = st * jnp.exp(g_end[:, sl]) + upd

    @pl.when(c_idx == pl.num_programs(1) - 1)
    def _():
        st_ref[...] = s_scr[...]


def _gla_prompt(q, k, v, r, lg, onorm, bb):
    bsz, t, hdk = q.shape
    hdv = v.shape[2]
    dk, dv = hdk // H_A, hdv // H_A
    c = min(CHUNK, t)
    assert t % c == 0 and bsz % bb == 0 and LANES % dk == 0
    n_pairs = hdk // LANES
    blk = lambda n: pl.BlockSpec((bb, c, n), lambda i, j: (i, j, 0))
    o, st = pl.pallas_call(
        functools.partial(_gla_kernel, bb=bb, dk=dk, dv=dv),
        grid=(bsz // bb, t // c),
        in_specs=[blk(hdk), blk(hdk), blk(hdv), blk(hdv), blk(hdk), _const_spec(onorm.shape)],
        out_specs=[blk(hdv), pl.BlockSpec((bb, n_pairs, dv, LANES), lambda i, j: (i, 0, 0, 0))],
        out_shape=[jax.ShapeDtypeStruct((bsz, t, hdv), BF16), jax.ShapeDtypeStruct((bsz, n_pairs, dv, LANES), F32)],
        scratch_shapes=[pltpu.VMEM((bb, n_pairs, dv, LANES), F32)],
        compiler_params=_cparams("parallel", "arbitrary"), name="gla_prompt")(q, k, v, r, lg, onorm)
    hpp = LANES // dk
    s = st.reshape(bsz, n_pairs, dv, hpp, dk).transpose(0, 1, 3, 4, 2).reshape(bsz, H_A, dk, dv)
    return o, s


def _unit_lower_inverse(a_strict, c):
    r = lax.broadcasted_iota(jnp.int32, (c, c), 0)
    s = lax.broadcasted_iota(jnp.int32, (c, c), 1)
    eye = jnp.where(r == s, 1.0, 0.0)
    inv = eye - jnp.where((r // 2) == (s // 2), a_strict, 0.0)
    w = 2
    while w < c:
        lvl = jnp.where(((r // (2 * w)) == (s // (2 * w))) & ((r // w) != (s // w)), a_strict, 0.0)
        inv = inv - _dot_hi(_dot_hi(inv, lvl), inv)
        w *= 2
    return inv


def _gdn_kernel(x_ref, z_ref, ab_ref, cw_ref, alog_ref, dtb_ref, on_ref, o_ref, s_ref, tail_ref, xbuf, s_scr,
                *, bb, n_heads):
    c_idx = pl.program_id(1)
    c = x_ref.shape[1]
    hd = z_ref.shape[2]
    dk = hd // n_heads
    pad = 8

    @pl.when(c_idx == 0)
    def _():
        s_scr[...] = jnp.zeros_like(s_scr)
        xbuf[:, 0:pad, :] = jnp.zeros((bb, pad, xbuf.shape[2]), F32)

    incl = _tri_incl(c)
    r = lax.broadcasted_iota(jnp.int32, (c, c), 0)
    s = lax.broadcasted_iota(jnp.int32, (c, c), 1)
    strict = r > s
    tri01 = jnp.where(incl, 1.0, 0.0).astype(BF16)
    cw = cw_ref[...]
    onorm = on_ref[...]
    neg_a = -jnp.exp(alog_ref[...])
    dtb = dtb_ref[...]
    for b in range(bb):
        x = x_ref[b]
        xbuf[b, pad:pad + c, :] = x
        conv = x * cw[CONV_W - 1:CONV_W, :]
        for j in range(1, CONV_W):
            conv = conv + xbuf[b, pad - j:pad - j + c, :] * cw[CONV_W - 1 - j:CONV_W - j, :]
        xbuf[b, 0:pad, :] = x[c - pad:c, :]
        act = _silu(conv)
        ab = ab_ref[b]
        lgb = neg_a * _softplus(ab + dtb)
        beta = _sigmoid(ab)
        g = _mask_dot(tri01, lgb)
        g_t = g.T
        for h in range(n_heads):
            hs = slice(h * dk, (h + 1) * dk)
            qh = act[:, hs]
            kh = act[:, hd + h * dk:hd + (h + 1) * dk]
            vh = act[:, 2 * hd + h * dk:2 * hd + (h + 1) * dk]
            qh = qh * lax.rsqrt(jnp.sum(qh * qh, axis=-1, keepdims=True) + EPS) * (dk ** -0.5)
            kh = kh * lax.rsqrt(jnp.sum(kh * kh, axis=-1, keepdims=True) + EPS)
            gc = g[:, h:h + 1]
            gr = g_t[h:h + 1, :]
            bcol = beta[:, n_heads + h:n_heads + h + 1]
            rel = jnp.where(incl, jnp.exp(jnp.minimum(gc - gr, 0.0)), 0.0)
            kk = _dot_nt(kh, kh)
            a_strict = jnp.where(strict, kk * rel, 0.0) * bcol
            t_inv = _unit_lower_inverse(a_strict, c)
            eg = jnp.exp(gc)
            w = _dot_hi(t_inv, kh * (bcol * eg))
            u = _dot_hi(t_inv, vh * bcol)
            st = s_scr[b, h]
            st_bf = st.astype(BF16)
            v_new = u - _dot(w, st_bf)
            qk = _dot_nt(qh, kh) * rel
            o = _dot(qk, v_new) + _dot(qh * eg, st_bf)
            g_end = g[c - 1:c, h:h + 1]
            s_scr[b, h] = st * jnp.exp(g_end) + _dot_tn(kh * jnp.exp(g_end - gc), v_new)
            o = _rms(o, onorm) * _silu(z_ref[b, :, hs])
            o_ref[b, :, hs] = o.astype(o_ref.dtype)

    @pl.when(c_idx == pl.num_programs(1) - 1)
    def _():
        s_ref[...] = s_scr[...]
        tail_ref[...] = xbuf[:, 0:pad, :]


def _gdn_prompt(qkv, z, ab, conv_w, a_log, dt_bias, onorm, bb):
    bsz, t, n3 = qkv.shape
    hd = z.shape[2]
    dk = hd // H_B
    c = min(CHUNK, t)
    assert t % c == 0 and bsz % bb == 0 and c >= 8
    blk = lambda n: pl.BlockSpec((bb, c, n), lambda i, j: (i, j, 0))
    o, s, tail = pl.pallas_call(
        functools.partial(_gdn_kernel, bb=bb, n_heads=H_B),
        grid=(bsz // bb, t // c),
        in_specs=[blk(n3), blk(hd), blk(LANES), _const_spec(conv_w.shape), _const_spec(a_log.shape),
                  _const_spec(dt_bias.shape), _const_spec(onorm.shape)],
        out_specs=[blk(hd), pl.BlockSpec((bb, H_B, dk, dk), lambda i, j: (i, 0, 0, 0)),
                   pl.BlockSpec((bb, 8, n3), lambda i, j: (i, 0, 0))],
        out_shape=[jax.ShapeDtypeStruct((bsz, t, hd), BF16), jax.ShapeDtypeStruct((bsz, H_B, dk, dk), F32),
                   jax.ShapeDtypeStruct((bsz, 8, n3), F32)],
        scratch_shapes=[pltpu.VMEM((bb, 8 + c, n3), F32), pltpu.VMEM((bb, H_B, dk, dk), F32)],
        compiler_params=_cparams("parallel", "arbitrary"), name="gdn_prompt")(qkv, z, ab, conv_w, a_log, dt_bias, onorm)
    return o, s, tail[:, 8 - (CONV_W - 1):, :]


def _sb_tile(qm, kblk, vblk, bias, u01, r_col, mask):
    z = _dot_nt(qm, kblk) + bias
    lg1m = -_softplus(z)
    if mask is not None:
        lg1m = jnp.where(mask, lg1m, 0.0)
    hi, lo = _split2(lg1m)
    inner = jnp.dot(hi, u01, preferred_element_type=F32) + jnp.dot(lo, u01, preferred_element_type=F32)
    wts = jnp.exp((z + lg1m) + (inner + r_col))
    if mask is not None:
        wts = jnp.where(mask, wts, 0.0)
    return _dot(wts, vblk), r_col + jnp.sum(lg1m, axis=1, keepdims=True)


def _sb_kernel(bias_ref, q_ref, k_ref, v_ref, o_ref, kb, vm, acc, rsum, *, hd, tk):
    p = pl.program_id(1)
    qi = pl.program_id(2)
    tq = q_ref.shape[1]
    t = k_ref.shape[1]
    hpp = LANES // hd
    lane_head_k = lax.broadcasted_iota(jnp.int32, (t, LANES), 1) // hd

    @pl.when(qi == 0)
    def _():
        kb[...] = k_ref[0].astype(BF16)
        v = v_ref[0]
        for hh in range(hpp):
            vm[hh] = jnp.where(lane_head_k == hh, v, 0.0).astype(BF16)

    lane_head_q = lax.broadcasted_iota(jnp.int32, (tq, LANES), 1) // hd
    qs = q_ref[0] * (hd ** -0.5)
    qm = [jnp.where(lane_head_q == hh, qs, 0.0).astype(BF16) for hh in range(hpp)]
    bias = [bias_ref[p * hpp + hh] for hh in range(hpp)]
    rr = lax.broadcasted_iota(jnp.int32, (tk, tk), 0)
    ss = lax.broadcasted_iota(jnp.int32, (tk, tk), 1)
    u01 = jnp.where(rr > ss, 1.0, 0.0).astype(BF16)
    n_sub = tq // tk

    acc[...] = jnp.zeros_like(acc)
    rsum[...] = jnp.zeros_like(rsum)

    def tile(start, mask):
        kblk = kb[pl.ds(start, tk), :]
        for hh in range(hpp):
            pv, r_new = _sb_tile(qm[hh], kblk, vm[hh, pl.ds(start, tk), :], bias[hh], u01, rsum[hh], mask)
            acc[...] += pv
            rsum[hh] = r_new

    qpos = qi * tq + lax.broadcasted_iota(jnp.int32, (tq, tk), 0)
    for sub in range(n_sub - 1, -1, -1):
        kpos = qi * tq + sub * tk + lax.broadcasted_iota(jnp.int32, (tq, tk), 1)
        tile(pl.multiple_of(qi * tq + sub * tk, tk), kpos < qpos)

    def body(i, carry):
        start = pl.multiple_of((qi * n_sub - 1 - i) * tk, tk)
        tile(start, None)
        return carry

    lax.fori_loop(0, qi * n_sub, body, 0)
    o_ref[0] = acc[...].astype(o_ref.dtype)


def _sb_prompt(q, k, v, bias, tq=256, tk=256):
    bsz, t, hhd = q.shape
    hd = hhd // H_C
    tq, tk = min(tq, t), min(tk, t)
    assert t % tq == 0 and tq % tk == 0 and LANES % hd == 0
    n_pairs = hhd // LANES
    hpp = LANES // hd
    return pl.pallas_call(
        functools.partial(_sb_kernel, hd=hd, tk=tk),
        grid=(bsz, n_pairs, t // tq),
        in_specs=[pl.BlockSpec(memory_space=pltpu.SMEM),
                  pl.BlockSpec((1, tq, LANES), lambda b, p, i: (b, i, p)),
                  pl.BlockSpec((1, t, LANES), lambda b, p, i: (b, 0, p)),
                  pl.BlockSpec((1, t, LANES), lambda b, p, i: (b, 0, p))],
        out_specs=pl.BlockSpec((1, tq, LANES), lambda b, p, i: (b, i, p)),
        out_shape=jax.ShapeDtypeStruct((bsz, t, hhd), BF16),
        scratch_shapes=[pltpu.VMEM((t, LANES), BF16), pltpu.VMEM((hpp, t, LANES), BF16),
                        pltpu.VMEM((tq, LANES), F32), pltpu.VMEM((hpp, tq, 1), F32)],
        compiler_params=_cparams("parallel", "parallel", "arbitrary"), name="sb_prompt")(bias, q, k, v)


def _rg_kernel(x_ref, y_ref, cw_ref, cb_ref, wa_ref, ba_ref, wx_ref, bx_ref, lam_ref,
               o_ref, h_ref, tail_ref, xbuf, a_scr, b_scr, h_scr):
    ti = pl.program_id(1)
    tt = x_ref.shape[1]
    d = x_ref.shape[2]
    pad = 8

    @pl.when(ti == 0)
    def _():
        h_scr[...] = jnp.zeros_like(h_scr)
        xbuf[0:pad, :] = jnp.zeros((pad, d), F32)

    x = x_ref[0]
    cw = cw_ref[...]
    xbuf[pad:pad + tt, :] = x
    conv = x * cw[CONV_W - 1:CONV_W, :]
    for j in range(1, CONV_W):
        conv = conv + xbuf[pad - j:pad - j + tt, :] * cw[CONV_W - 1 - j:CONV_W - j, :]
    xbuf[0:pad, :] = x[tt - pad:tt, :]
    xc = conv + cb_ref[...]
    nsp = _softplus(-lam_ref[...])
    xb = xc.astype(BF16)
    rg = _sigmoid(jnp.dot(xb, wa_ref[...], preferred_element_type=F32) + ba_ref[...])
    ig = _sigmoid(jnp.dot(xb, wx_ref[...], preferred_element_type=F32) + bx_ref[...])
    log_a = (-RG_C) * rg * nsp
    a = jnp.exp(log_a)
    bt = jnp.sqrt(_one_minus_exp2(log_a, a)) * (ig * xc)
    pos = ti * tt + lax.broadcasted_iota(jnp.int32, (tt, d), 0)
    a = jnp.where(pos == 0, 0.0, a)
    bt = jnp.where(pos == 0, ig * xc, bt)

    row8 = lax.broadcasted_iota(jnp.int32, (tt, d), 0) % 8
    for sh in (1, 2, 4):
        a_prev = jnp.where(row8 >= sh, pltpu.roll(a, sh, 0), 1.0)
        b_prev = jnp.where(row8 >= sh, pltpu.roll(bt, sh, 0), 0.0)
        bt = a * b_prev + bt
        a = a * a_prev
    a_scr[...] = a
    b_scr[...] = bt

    def body(i, h):
        r0 = pl.multiple_of(i * 8, 8)
        hg = a_scr[pl.ds(r0, 8), :] * h + b_scr[pl.ds(r0, 8), :]
        b_scr[pl.ds(r0, 8), :] = hg
        return hg[7:8, :]

    h_last = lax.fori_loop(0, tt // 8, body, h_scr[...])
    h_scr[...] = h_last
    hs = b_scr[...]
    o_ref[0] = (hs * _gelu_tanh(y_ref[0])).astype(o_ref.dtype)

    @pl.when(ti == pl.num_programs(1) - 1)
    def _():
        h_ref[0] = h_last
        tail_ref[0] = xbuf[0:pad, :]


def _rg_prompt(xr, yr, conv_w, conv_b, wa_bd, b_a, wx_bd, b_x, lam, tt=512):
    bsz, t, d = xr.shape
    tt = min(tt, t)
    assert t % tt == 0 and tt % 8 == 0
    blk = pl.BlockSpec((1, tt, d), lambda b, i: (b, i, 0))
    consts = (conv_w, conv_b, wa_bd, b_a, wx_bd, b_x, lam)
    o, h, tail = pl.pallas_call(
        _rg_kernel,
        grid=(bsz, t // tt),
        in_specs=[blk, blk] + [_const_spec(a.shape) for a in consts],
        out_specs=[blk, pl.BlockSpec((1, 1, d), lambda b, i: (b, 0, 0)), pl.BlockSpec((1, 8, d), lambda b, i: (b, 0, 0))],
        out_shape=[jax.ShapeDtypeStruct((bsz, t, d), BF16), jax.ShapeDtypeStruct((bsz, 1, d), F32),
                   jax.ShapeDtypeStruct((bsz, 8, d), F32)],
        scratch_shapes=[pltpu.VMEM((8 + tt, d), F32), pltpu.VMEM((tt, d), F32), pltpu.VMEM((tt, d), F32),
                        pltpu.VMEM((1, d), F32)],
        compiler_params=_cparams("parallel", "arbitrary"), name="rglru_prompt")(xr, yr, *consts)
    return o, h[:, 0, :], tail[:, 8 - (CONV_W - 1):, :]


def _even_prep_kernel(x_ref, hist_ref, cw_ref, ab_ref, alog_ref, dtb_ref, lgp_ref,
                      q_ref, k_ref, v_ref, nh_ref, gb_ref, lg_ref, *, n_heads):
    x = x_ref[...]
    cw = cw_ref[...]
    conv = x * cw[CONV_W - 1:CONV_W, :]
    for j in range(CONV_W - 1):
        conv = conv + hist_ref[j] * cw[j:j + 1, :]
    for j in range(CONV_W - 2):
        nh_ref[j] = hist_ref[j + 1]
    nh_ref[CONV_W - 2] = x
    act = _silu(conv)
    hd = q_ref.shape[1]
    dk = hd // n_heads
    for h in range(n_heads):
        hs = slice(h * dk, (h + 1) * dk)
        qh = act[:, hs]
        kh = act[:, hd + h * dk:hd + (h + 1) * dk]
        q_ref[:, hs] = qh * lax.rsqrt(jnp.sum(qh * qh, axis=-1, keepdims=True) + EPS) * (dk ** -0.5)
        k_ref[:, hs] = kh * lax.rsqrt(jnp.sum(kh * kh, axis=-1, keepdims=True) + EPS)
    v_ref[...] = act[:, 2 * hd:3 * hd]
    ab = ab_ref[...]
    lane = lax.broadcasted_iota(jnp.int32, ab.shape, 1)
    lgb = -jnp.exp(alog_ref[...]) * _softplus(ab + dtb_ref[...])
    gb_ref[...] = jnp.where(lane < n_heads, lgb, _sigmoid(ab))
    lg_ref[...] = -_softplus(-lgp_ref[...]) * (1.0 / GLA_TAU)


def _even_state_kernel(qa_ref, ka_ref, lga_ref, va_ref, qb_ref, kb_ref, vb_ref, gb_ref, sa_ref, sb_ref,
                       oa_ref, ob_ref, sa_out, sb_out, *, nb, dk_a, dv_a, dk_b):
    for j in range(nb):
        gbrow = gb_ref[j:j + 1, :]
        for h in range(H_A):
            rs = slice(h * dk_a, (h + 1) * dk_a)
            s_new = sa_ref[j, h] * jnp.exp(lga_ref[0, rs, j:j + 1]) + ka_ref[0, rs, j:j + 1] * va_ref[j:j + 1, h * dv_a:(h + 1) * dv_a]
            sa_out[j, h] = s_new
            oa_ref[j:j + 1, h * dv_a:(h + 1) * dv_a] = jnp.sum(qa_ref[0, rs, j:j + 1] * (dk_a ** -0.5) * s_new, axis=0, keepdims=True)
        for h in range(H_B):
            rs = slice(h * dk_b, (h + 1) * dk_b)
            st = sb_ref[j, h]
            eg = jnp.exp(gbrow[:, h:h + 1])
            beta = gbrow[:, H_B + h:H_B + h + 1]
            kcol = kb_ref[0, rs, j:j + 1]
            vrow = vb_ref[j:j + 1, rs]
            v_new = vrow * beta - jnp.sum((kcol * (beta * eg)) * st, axis=0, keepdims=True)
            s_new = st * eg + kcol * v_new
            sb_out[j, h] = s_new
            ob_ref[j:j + 1, rs] = jnp.sum(qb_ref[0, rs, j:j + 1] * s_new, axis=0, keepdims=True)


def _even_out_kernel(oa_ref, ob_ref, r_ref, z_ref, ona_ref, onb_ref, o_ref, *, dv_a, dv_b):
    n_a = oa_ref.shape[1]
    for h in range(n_a // dv_a):
        sl = slice(h * dv_a, (h + 1) * dv_a)
        o_ref[:, sl] = (_rms(oa_ref[:, sl], ona_ref[...]) * _silu(r_ref[:, sl])).astype(o_ref.dtype)
    for h in range(ob_ref.shape[1] // dv_b):
        sl = slice(h * dv_b, (h + 1) * dv_b)
        o_ref[:, n_a + h * dv_b:n_a + (h + 1) * dv_b] = (_rms(ob_ref[:, sl], onb_ref[...]) * _silu(z_ref[:, sl])).astype(o_ref.dtype)


def _cols(x, nb):
    n, d = x.shape
    return x.reshape(n // nb, nb, d).transpose(0, 2, 1)


def _even_sample(qa, ka, va, ra, lgp, qkv, zb, ab, s_gla, s_gdn, hist, conv_w, a_log, dt_bias, on_a, on_b, nb=8):
    n, hdk_a = qa.shape
    hd_b = zb.shape[1]
    n3 = qkv.shape[1]
    hist_t = hist.transpose(1, 0, 2)
    full = lambda a: pl.BlockSpec(a.shape, lambda *_: (0,) * a.ndim)
    sds = lambda *s: jax.ShapeDtypeStruct(s, F32)
    ins = (qkv, hist_t, conv_w, ab, a_log, dt_bias, lgp)
    outs = [sds(n, hd_b), sds(n, hd_b), sds(n, hd_b), sds(CONV_W - 1, n, n3), sds(n, LANES), sds(n, hdk_a)]
    qb, kb, vb, new_hist_t, gb, lga = pl.pallas_call(
        functools.partial(_even_prep_kernel, n_heads=H_B),
        in_specs=[full(a) for a in ins], out_specs=[full(o) for o in outs], out_shape=outs,
        compiler_params=pltpu.CompilerParams(vmem_limit_bytes=VMEM_LIMIT_BYTES), name="even_sample_prep")(*ins)

    dk_a, dv_a, dk_b = hdk_a // H_A, va.shape[1] // H_A, hd_b // H_B
    assert n % nb == 0
    colspec = lambda d: pl.BlockSpec((1, d, nb), lambda i: (i, 0, 0))
    rowspec = lambda d: pl.BlockSpec((nb, d), lambda i: (i, 0))
    sa_spec = pl.BlockSpec((nb, H_A, dk_a, dv_a), lambda i: (i, 0, 0, 0))
    sb_spec = pl.BlockSpec((nb, H_B, dk_b, dk_b), lambda i: (i, 0, 0, 0))
    oa, ob, s_gla_new, s_gdn_new = pl.pallas_call(
        functools.partial(_even_state_kernel, nb=nb, dk_a=dk_a, dv_a=dv_a, dk_b=dk_b),
        grid=(n // nb,),
        in_specs=[colspec(hdk_a), colspec(hdk_a), colspec(hdk_a), rowspec(va.shape[1]),
                  colspec(hd_b), colspec(hd_b), rowspec(hd_b), rowspec(LANES), sa_spec, sb_spec],
        out_specs=[rowspec(va.shape[1]), rowspec(hd_b), sa_spec, sb_spec],
        out_shape=[sds(n, va.shape[1]), sds(n, hd_b), sds(*s_gla.shape), sds(*s_gdn.shape)],
        compiler_params=_cparams("parallel"), name="even_sample_state")(
            _cols(qa, nb), _cols(ka, nb), _cols(lga, nb), va, _cols(qb, nb), _cols(kb, nb), vb, gb, s_gla, s_gdn)

    ins = (oa, ob, ra, zb, on_a, on_b)
    out = jax.ShapeDtypeStruct((n, va.shape[1] + hd_b), BF16)
    o = pl.pallas_call(
        functools.partial(_even_out_kernel, dv_a=dv_a, dv_b=dk_b),
        in_specs=[full(a) for a in ins], out_specs=full(out), out_shape=out,
        compiler_params=pltpu.CompilerParams(vmem_limit_bytes=VMEM_LIMIT_BYTES), name="even_sample_out")(*ins)
    return o, s_gla_new, s_gdn_new, new_hist_t.transpose(1, 0, 2)


def _sb_decode_kernel(pt_ref, bias_ref, q_ref, *rest, n_pages, hd):
    k_refs = rest[:n_pages]
    v_refs = rest[n_pages:2 * n_pages]
    o_ref = rest[2 * n_pages]
    page = k_refs[0].shape[1]
    hhd = q_ref.shape[2]
    n_heads = hhd // hd
    hrow = lax.broadcasted_iota(jnp.int32, (n_heads, hhd), 0)
    lane_head = lax.broadcasted_iota(jnp.int32, (n_heads, hhd), 1) // hd
    own = hrow == lane_head
    qrows = jnp.where(own, q_ref[0] * (hd ** -0.5), 0.0).astype(BF16)
    rr = lax.broadcasted_iota(jnp.int32, (page, page), 0)
    ss = lax.broadcasted_iota(jnp.int32, (page, page), 1)
    u01 = jnp.where(rr > ss, 1.0, 0.0).astype(BF16)
    bias = bias_ref[...]
    r_col = jnp.zeros((n_heads, 1), F32)
    acc = jnp.zeros((n_heads, hhd), F32)
    for pg in range(n_pages - 1, -1, -1):
        z = _dot_nt(qrows, k_refs[pg][0]) + bias
        lg1m = -_softplus(z)
        hi, lo = _split2(lg1m)
        inner = jnp.dot(hi, u01, preferred_element_type=F32) + jnp.dot(lo, u01, preferred_element_type=F32)
        wts = jnp.exp((z + lg1m) + (inner + r_col))
        acc = acc + _dot(wts, v_refs[pg][0])
        r_col = r_col + jnp.sum(lg1m, axis=1, keepdims=True)
    o_ref[0] = jnp.sum(jnp.where(own, acc, 0.0), axis=0, keepdims=True)


def _sb_decode(q, cache_k, cache_v, page_table, bias):
    n, hhd = q.shape
    n_pool, page = cache_k.shape[0], cache_k.shape[1]
    n_pages = page_table.shape[1]
    hd = hhd // H_C
    kc = cache_k.reshape(n_pool, page, hhd)
    vc = cache_v.reshape(n_pool, page, hhd)
    pt = page_table.reshape(-1)

    def page_spec(pg):
        return pl.BlockSpec((1, page, hhd), lambda b, pt_ref: (pt_ref[b * n_pages + pg], 0, 0))

    grid_spec = pltpu.PrefetchScalarGridSpec(
        num_scalar_prefetch=1, grid=(n,),
        in_specs=[pl.BlockSpec((H_C, 1), lambda b, pt_ref: (0, 0)),
                  pl.BlockSpec((1, 1, hhd), lambda b, pt_ref: (b, 0, 0))]
                 + [page_spec(pg) for pg in range(n_pages)] * 2,
        out_specs=pl.BlockSpec((1, 1, hhd), lambda b, pt_ref: (b, 0, 0)))
    o = pl.pallas_call(
        functools.partial(_sb_decode_kernel, n_pages=n_pages, hd=hd),
        grid_spec=grid_spec, out_shape=jax.ShapeDtypeStruct((n, 1, hhd), F32),
        compiler_params=_cparams("parallel"), name="sb_decode")(
            pt, bias.reshape(H_C, 1), q.reshape(n, 1, hhd), *([kc] * n_pages), *([vc] * n_pages))
    return o.reshape(n, hhd)


def _rg_decode_kernel(x_ref, y_ref, oc_ref, hist_ref, h0_ref, cw_ref, cb_ref, wa_ref, ba_ref, wx_ref, bx_ref, lam_ref,
                      o_ref, h_ref, nh_ref, *, reset):
    x = x_ref[...]
    cw = cw_ref[...]
    conv = x * cw[CONV_W - 1:CONV_W, :]
    for j in range(CONV_W - 1):
        conv = conv + hist_ref[j] * cw[j:j + 1, :]
    for j in range(CONV_W - 2):
        nh_ref[j] = hist_ref[j + 1]
    nh_ref[CONV_W - 2] = x
    xc = conv + cb_ref[...]
    xb = xc.astype(BF16)
    rg = _sigmoid(jnp.dot(xb, wa_ref[...], preferred_element_type=F32) + ba_ref[...])
    ig = _sigmoid(jnp.dot(xb, wx_ref[...], preferred_element_type=F32) + bx_ref[...])
    log_a = (-RG_C) * rg * _softplus(-lam_ref[...])
    if reset:
        h = ig * xc
    else:
        a = jnp.exp(log_a)
        h = a * h0_ref[...] + jnp.sqrt(_one_minus_exp2(log_a, a)) * (ig * xc)
    h_ref[...] = h
    d_c = oc_ref.shape[1]
    o_ref[:, 0:d_c] = oc_ref[...].astype(o_ref.dtype)
    o_ref[:, d_c:] = (h * _gelu_tanh(y_ref[...])).astype(o_ref.dtype)


def _odd_sample(oc, xr, yr, hist, h0, conv_w, conv_b, wa_bd, b_a, wx_bd, b_x, lam, offset):
    n, d = xr.shape
    hist_t = hist.transpose(1, 0, 2)
    full = lambda a: pl.BlockSpec(a.shape, lambda *_: (0,) * a.ndim)
    ins = (xr, yr, oc, hist_t, h0, conv_w, conv_b, wa_bd, b_a, wx_bd, b_x, lam)
    outs = [jax.ShapeDtypeStruct((n, oc.shape[1] + d), BF16), jax.ShapeDtypeStruct((n, d), F32),
            jax.ShapeDtypeStruct((CONV_W - 1, n, d), F32)]
    o, h, nh = pl.pallas_call(
        functools.partial(_rg_decode_kernel, reset=(offset == 0)),
        in_specs=[full(a) for a in ins], out_specs=[full(a) for a in outs], out_shape=outs,
        compiler_params=pltpu.CompilerParams(vmem_limit_bytes=VMEM_LIMIT_BYTES), name="rglru_decode")(*ins)
    return o, h, nh.transpose(1, 0, 2)


def _pad_cols(w, n):
    return jnp.pad(w, ((0, 0), (0, n - w.shape[1])))


def _block_diag(w):
    h, a, b = w.shape
    eye = jnp.eye(h, dtype=w.dtype)
    return (eye[:, None, :, None] * w[:, :, None, :]).reshape(h * a, h * b)


def _row(v, n=None):
    v = v.reshape(1, -1).astype(F32)
    return v if n is None else _pad_cols(v, n)


def kernel(x_prompt, x_sample, state_gla, state_gdn, state_gdn_conv, cache_sb_k, cache_sb_v, state_rg_h, state_rg_conv, page_table, ln_ffn, ffn_w_gate, ffn_w_up, ffn_w_down, ln_mix_e, w_in_e, gla_w_gate, gla_b_gate, gla_onorm, gdn_conv_w, gdn_a_log, gdn_dt_bias, gdn_onorm, w_out_e, ln_mix_o, w_in_o, sb_bias, rg_conv_w, rg_conv_b, rg_w_a, rg_b_a, rg_w_x, rg_b_x, rg_lambda, w_out_o, ln_final):
    bp, t, d = x_prompt.shape
    bs, ts, _ = x_sample.shape
    assert ts == 1
    depth = ln_ffn.shape[0]
    dk_a, dv_a = state_gla.shape[3], state_gla.shape[4]
    dk_b = state_gdn.shape[3]
    hd_c = cache_sb_k.shape[4]
    d_rnn = state_rg_h.shape[2]
    rank = gla_w_gate.shape[1]
    n_qa, n_va, n_b, n_c = H_A * dk_a, H_A * dv_a, H_B * dk_b, H_C * hd_c
    offset = page_table.shape[1] * cache_sb_k.shape[2]

    xp = x_prompt.reshape(bp * t, d)
    xs = x_sample.reshape(bs, d)
    outs = {}
    mix_p = mix_s = None
    for layer in range(depth):
        ffn = [(_row(ln_ffn[layer, i]), ffn_w_gate[layer, i].astype(BF16), ffn_w_up[layer, i].astype(BF16),
                ffn_w_down[layer, i].astype(BF16)) for i in range(2)]
        if layer % 2 == 0:
            e = layer // 2
            w = w_in_e[e]
            c0 = 0
            cols = []
            for n in (n_qa, n_qa, n_va, n_va, rank, 3 * n_b, n_b, 2 * H_B):
                cols.append(w[:, c0:c0 + n])
                c0 += n
            w_q, w_k, w_v, w_r, w_la, w_qkv, w_z, w_ab = cols
            w_list = [a.astype(BF16) for a in (w_q, w_k, w_v, w_r, w_qkv, w_z, _pad_cols(w_ab, LANES), _pad_cols(w_la, LANES))]
            gate = (jnp.pad(gla_w_gate[e], ((0, LANES - rank), (0, 0))).astype(BF16), _row(gla_b_gate[e]))
            proj = (_row(ln_mix_e[e]), w_list)
            w_out = w_out_e[e].astype(BF16)
        else:
            o = layer // 2
            w = w_in_o[o]
            w_list = [w[:, i * n_c:(i + 1) * n_c].astype(BF16) for i in range(3)]
            w_list += [w[:, 3 * n_c:3 * n_c + d_rnn].astype(BF16), w[:, 3 * n_c + d_rnn:].astype(BF16)]
            gate = None
            proj = (_row(ln_mix_o[o]), w_list)
            w_out = w_out_o[o].astype(BF16)

        xp, *pp = _block_call(xp, None, ffn[0], proj=proj, gate=gate)
        xs, *ps = _block_call(xs, None, ffn[0], proj=proj, gate=gate)

        if layer % 2 == 0:
            e = layer // 2
            r3 = lambda a: a.reshape(bp, t, a.shape[1])
            qa, ka, va, ra, qkv, zb, ab, lgp = pp
            on_a, on_b = _row(gla_onorm[e]), _row(gdn_onorm[e])
            a_log, dtb = _row(gdn_a_log[e], LANES), _row(gdn_dt_bias[e], LANES)
            oa, s_gla_p = _gla_prompt(r3(qa), r3(ka), r3(va), r3(ra), r3(lgp), on_a, bb=bp)
            ob, s_gdn_p, tail_p = _gdn_prompt(r3(qkv), r3(zb), r3(ab), gdn_conv_w[e], a_log, dtb, on_b, bb=bp)
            mix_p = ([oa.reshape(bp * t, n_va), ob.reshape(bp * t, n_b)], [w_out[:n_va], w_out[n_va:]])
            qa, ka, va, ra, qkv, zb, ab, lgp = ps
            o_s, s_gla_s, s_gdn_s, hist_s = _even_sample(qa, ka, va, ra, lgp, qkv, zb, ab, state_gla[e], state_gdn[e],
                                                         state_gdn_conv[e], gdn_conv_w[e], a_log, dtb, on_a, on_b)
            mix_s = ([o_s], [w_out])
            outs.setdefault("gla_p", []).append(s_gla_p)
            outs.setdefault("gla_s", []).append(s_gla_s)
            outs.setdefault("gdn_p", []).append(s_gdn_p)
            outs.setdefault("gdn_s", []).append(s_gdn_s)
            outs.setdefault("gcv_p", []).append(tail_p)
            outs.setdefault("gcv_s", []).append(hist_s)
        else:
            o = layer // 2
            r3 = lambda a: a.reshape(bp, t, a.shape[1])
            qc, kc, vc, xr, yr = pp
            wa_bd, wx_bd = _block_diag(rg_w_a[o]).astype(BF16), _block_diag(rg_w_x[o]).astype(BF16)
            rg_consts = (rg_conv_w[o], _row(rg_conv_b[o]), wa_bd, _row(rg_b_a[o]), wx_bd, _row(rg_b_x[o]), _row(rg_lambda[o]))
            oc = _sb_prompt(r3(qc), r3(kc), r3(vc), sb_bias[o])
            od, h_p, tail_p = _rg_prompt(r3(xr), r3(yr), *rg_consts)
            mix_p = ([oc.reshape(bp * t, n_c), od.reshape(bp * t, d_rnn)], [w_out[:n_c], w_out[n_c:]])
            outs.setdefault("sbk_p", []).append(kc.reshape(bp, t, H_C, hd_c))
            outs.setdefault("sbv_p", []).append(vc.reshape(bp, t, H_C, hd_c))
            outs.setdefault("rgh_p", []).append(h_p)
            outs.setdefault("rgc_p", []).append(tail_p)
            qc, kc, vc, xr, yr = ps
            oc_s = _sb_decode(qc, cache_sb_k[o], cache_sb_v[o], page_table, sb_bias[o])
            o_s, h_s, hist_s = _odd_sample(oc_s, xr, yr, state_rg_conv[o], state_rg_h[o], *rg_consts, offset)
            mix_s = ([o_s], [w_out])
            outs.setdefault("sbk_s", []).append(kc.reshape(bs, 1, H_C, hd_c))
            outs.setdefault("sbv_s", []).append(vc.reshape(bs, 1, H_C, hd_c))
            outs.setdefault("rgh_s", []).append(h_s)
            outs.setdefault("rgc_s", []).append(hist_s)

        final_g = _row(ln_final) if layer == depth - 1 else None
        (xp,) = _block_call(xp, mix_p, ffn[1], final_g=final_g)
        (xs,) = _block_call(xs, mix_s, ffn[1], final_g=final_g)

    st = lambda name: jnp.stack(outs[name])
    return (xp.reshape(bp, t, d), xs.reshape(bs, 1, d), st("gla_p"), st("gla_s"), st("gdn_p"), st("gdn_s"),
            st("gcv_p"), st("gcv_s"), st("sbk_p"), st("sbk_s"), st("sbv_p"), st("sbv_s"),
            st("rgh_p"), st("rgh_s"), st("rgc_p"), st("rgc_s"))
```

```python
import functools
import math

import jax
import jax.numpy as jnp
from jax import lax
from jax.experimental import pallas as pl
from jax.experimental.pallas import tpu as pltpu

F32 = jnp.float32
BF16 = jnp.bfloat16

H_A = 4
GLA_TAU = 16.0
H_B = 4
CONV_W = 4
H_C = 8
H_D = 8
RG_C = 8.0
CHUNK = 64
EPS = 1e-6
LOG2E = 1.4426950408889634

VMEM_LIMIT_BYTES = 56 * 1024 * 1024
LANES = 128


def _cparams(*sem):
    return pltpu.CompilerParams(dimension_semantics=sem, vmem_limit_bytes=VMEM_LIMIT_BYTES)


def _dot(a, b):
    return jnp.dot(a.astype(BF16), b.astype(BF16), preferred_element_type=F32)


def _dot_nt(a, b):
    return lax.dot_general(a.astype(BF16), b.astype(BF16), (((1,), (1,)), ((), ())), preferred_element_type=F32)


def _dot_tn(a, b):
    return lax.dot_general(a.astype(BF16), b.astype(BF16), (((0,), (0,)), ((), ())), preferred_element_type=F32)


def _split2(x):
    hi = x.astype(BF16)
    lo = (x - hi.astype(F32)).astype(BF16)
    return hi, lo


def _split3(x):
    x1 = x.astype(BF16)
    r1 = x - x1.astype(F32)
    x2 = r1.astype(BF16)
    x3 = (r1 - x2.astype(F32)).astype(BF16)
    return x1, x2, x3


def _dot_hi(a, b):
    a1, a2 = _split2(a)
    b1, b2 = _split2(b)
    d = lambda u, v: jnp.dot(u, v, preferred_element_type=F32)
    return d(a1, b1) + (d(a1, b2) + d(a2, b1))


def _mask_dot(m01, x):
    x1, x2, x3 = _split3(x)
    d = lambda v: jnp.dot(m01, v, preferred_element_type=F32)
    return d(x1) + (d(x2) + d(x3))


def _rms(x, g):
    return x * lax.rsqrt(jnp.mean(x * x, axis=-1, keepdims=True) + EPS) * g


def _sigmoid(x):
    return 1.0 / (1.0 + jnp.exp(-x))


def _silu(x):
    return x * _sigmoid(x)


def _softplus(x):
    return jnp.maximum(x, 0.0) + jnp.log1p(jnp.exp(-jnp.abs(x)))


def _gelu_tanh(x):
    return 0.5 * x * (1.0 + jnp.tanh(math.sqrt(2.0 / math.pi) * (x + 0.044715 * (x * x * x))))


def _one_minus_exp2(log_a, a):
    return jnp.tanh(-log_a) * (a * a + 1.0)


def _const_spec(shape):
    nd = len(shape)
    return pl.BlockSpec(shape, lambda *_: (0,) * nd, pipeline_mode=pl.Buffered(1))


def _block_kernel(*refs, n_mix, n_proj, has_gate, has_final, ff_chunk):
    it = iter(refs)
    x_ref = next(it)
    o_refs = [next(it) for _ in range(n_mix)]
    wo_refs = [next(it) for _ in range(n_mix)]
    gf_ref, wg_ref, wu_ref, wd_ref = next(it), next(it), next(it), next(it)
    if has_final:
        gfin_ref = next(it)
    if n_proj:
        gm_ref = next(it)
        wp_refs = [next(it) for _ in range(n_proj)]
    if has_gate:
        wgate_ref, bgate_ref = next(it), next(it)
    xo_ref = next(it)
    po_refs = [next(it) for _ in range(n_proj)]

    x = x_ref[...]
    for o_ref, wo_ref in zip(o_refs, wo_refs):
        x = x + _dot(o_ref[...], wo_ref[...])
    h = _rms(x, gf_ref[...]).astype(BF16)
    d_ff = wg_ref.shape[1]
    acc = jnp.zeros_like(x)
    for c0 in range(0, d_ff, ff_chunk):
        gt = jnp.dot(h, wg_ref[:, c0:c0 + ff_chunk], preferred_element_type=F32)
        up = jnp.dot(h, wu_ref[:, c0:c0 + ff_chunk], preferred_element_type=F32)
        act = (_silu(gt) * up).astype(BF16)
        acc = acc + jnp.dot(act, wd_ref[c0:c0 + ff_chunk, :], preferred_element_type=F32)
    x = x + 0.5 * acc
    if has_final:
        xo_ref[...] = _rms(x, gfin_ref[...])
    else:
        xo_ref[...] = x
    if n_proj:
        hm = _rms(x, gm_ref[...]).astype(BF16)
        for j, (wp_ref, po_ref) in enumerate(zip(wp_refs, po_refs)):
            p = jnp.dot(hm, wp_ref[...], preferred_element_type=F32)
            if has_gate and j == n_proj - 1:
                p = _dot(p, wgate_ref[...]) + bgate_ref[...]
            po_ref[...] = p


def _block_call(x, mix, ffn, final_g=None, proj=None, gate=None, tm=256):
    m, d = x.shape
    tm = min(tm, m)
    assert m % tm == 0
    g_ffn, wg, wu, wd = ffn
    d_ff = wg.shape[1]
    ff_chunk = d_ff // 2 if (d_ff // 2) % LANES == 0 else d_ff
    row = lambda n: pl.BlockSpec((tm, n), lambda i: (i, 0))
    args, specs = [x], [row(d)]
    o_list, wo_list = mix if mix is not None else ((), ())
    for o in o_list:
        args.append(o)
        specs.append(row(o.shape[1]))
    for w in wo_list:
        args.append(w)
        specs.append(_const_spec(w.shape))
    for a in (g_ffn, wg, wu, wd):
        args.append(a)
        specs.append(_const_spec(a.shape))
    if final_g is not None:
        args.append(final_g)
        specs.append(_const_spec(final_g.shape))
    out_shapes, out_specs = [jax.ShapeDtypeStruct((m, d), F32)], [row(d)]
    n_proj = 0
    if proj is not None:
        g_mix, w_list = proj
        n_proj = len(w_list)
        args.append(g_mix)
        specs.append(_const_spec(g_mix.shape))
        for w in w_list:
            args.append(w)
            specs.append(_const_spec(w.shape))
        widths = [w.shape[1] for w in w_list]
        if gate is not None:
            for a in gate:
                args.append(a)
                specs.append(_const_spec(a.shape))
            widths[-1] = gate[0].shape[1]
        for n in widths:
            out_shapes.append(jax.ShapeDtypeStruct((m, n), F32))
            out_specs.append(row(n))
    kern = functools.partial(_block_kernel, n_mix=len(o_list), n_proj=n_proj, has_gate=gate is not None,
                             has_final=final_g is not None, ff_chunk=ff_chunk)
    return pl.pallas_call(
        kern, grid=(m // tm,), in_specs=specs, out_specs=out_specs, out_shape=out_shapes,
        compiler_params=_cparams("parallel"), name="block")(*args)


def _tri_incl(c):
    r = lax.broadcasted_iota(jnp.int32, (c, c), 0)
    s = lax.broadcasted_iota(jnp.int32, (c, c), 1)
    return r >= s


def _gla_kernel(q_ref, k_ref, v_ref, r_ref, lg_ref, on_ref, o_ref, st_ref, s_scr, *, bb, dk, dv):
    c_idx = pl.program_id(1)
    c = q_ref.shape[1]
    n_pairs = q_ref.shape[2] // LANES
    hpp = LANES // dk

    @pl.when(c_idx == 0)
    def _():
        s_scr[...] = jnp.zeros_like(s_scr)

    incl = _tri_incl(c)
    tri01 = jnp.where(incl, 1.0, 0.0).astype(BF16)
    lane_head = lax.broadcasted_iota(jnp.int32, (c, LANES), 1) // dk
    onorm = on_ref[...]
    for b in range(bb):
        lg = -_softplus(-lg_ref[b]) * (1.0 / GLA_TAU)
        g = _mask_dot(tri01, lg)
        g_mid = g[c // 2 - 1:c // 2, :]
        g_end = g[c - 1:c, :]
        qs = q_ref[b] * (dk ** -0.5)
        kk = k_ref[b]
        q_state = qs * jnp.exp(g)
        q_in = qs * jnp.exp(g - g_mid)
        k_in = kk * jnp.exp(g_mid - g)
        k_end = kk * jnp.exp(g_end - g)
        for p in range(n_pairs):
            sl = slice(p * LANES, (p + 1) * LANES)
            st = s_scr[b, p]
            st_bf = st.astype(BF16)
            upd = None
            for hh in range(hpp):
                h = p * hpp + hh
                vs = slice(h * dv, (h + 1) * dv)
                sel = lane_head == hh
                att = _dot_nt(jnp.where(sel, q_in[:, sl], 0.0), k_in[:, sl])
                att = jnp.where(incl, att, 0.0)
                vh = v_ref[b, :, vs]
                o = _dot(att, vh) + _dot_nt(jnp.where(sel, q_state[:, sl], 0.0), st_bf)
                o = _rms(o, onorm) * _silu(r_ref[b, :, vs])
                o_ref[b, :, vs] = o.astype(o_ref.dtype)
                u = _dot_tn(vh, jnp.where(sel, k_end[:, sl], 0.0))
                upd = u if upd is None else upd + u
            s_scr[b, p] = st * jnp.exp(g_end[:, sl]) + upd

    @pl.when(c_idx == pl.num_programs(1) - 1)
    def _():
        st_ref[...] = s_scr[...]


def _gla_prompt(q, k, v, r, lg, onorm, bb):
    bsz, t, hdk = q.shape
    hdv = v.shape[2]
    dk, dv = hdk // H_A, hdv // H_A
    c = min(CHUNK, t)
    assert t % c == 0 and bsz % bb == 0 and LANES % dk == 0
    n_pairs = hdk // LANES
    blk = lambda n: pl.BlockSpec((bb, c, n), lambda i, j: (i, j, 0))
    o, st = pl.pallas_call(
        functools.partial(_gla_kernel, bb=bb, dk=dk, dv=dv),
        grid=(bsz // bb, t // c),
        in_specs=[blk(hdk), blk(hdk), blk(hdv), blk(hdv), blk(hdk), _const_spec(onorm.shape)],
        out_specs=[blk(hdv), pl.BlockSpec((bb, n_pairs, dv, LANES), lambda i, j: (i, 0, 0, 0))],
        out_shape=[jax.ShapeDtypeStruct((bsz, t, hdv), BF16), jax.ShapeDtypeStruct((bsz, n_pairs, dv, LANES), F32)],
        scratch_shapes=[pltpu.VMEM((bb, n_pairs, dv, LANES), F32)],
        compiler_params=_cparams("parallel", "arbitrary"), name="gla_prompt")(q, k, v, r, lg, onorm)
    hpp = LANES // dk
    s = st.reshape(bsz, n_pairs, dv, hpp, dk).transpose(0, 1, 3, 4, 2).reshape(bsz, H_A, dk, dv)
    return o, s


def _gdn_kernel(x_ref, z_ref, ab_ref, cw_ref, alog_ref, dtb_ref, on_ref, o_ref, s_ref, tail_ref, xbuf, s_scr,
                *, bb, n_heads):
    c_idx = pl.program_id(1)
    c = x_ref.shape[1]
    hd = z_ref.shape[2]
    dk = hd // n_heads
    pad = 8

    @pl.when(c_idx == 0)
    def _():
        s_scr[...] = jnp.zeros_like(s_scr)
        xbuf[:, 0:pad, :] = jnp.zeros((bb, pad, xbuf.shape[2]), F32)

    incl = _tri_incl(c)
    r = lax.broadcasted_iota(jnp.int32, (c, c), 0)
    s = lax.broadcasted_iota(jnp.int32, (c, c), 1)
    strict = r > s
    tri01 = jnp.where(incl, 1.0, 0.0).astype(BF16)
    cw = cw_ref[...]
    onorm = on_ref[...]
    neg_a = -jnp.exp(alog_ref[...])
    dtb = dtb_ref[...]
    probs = []
    for b in range(bb):
        x = x_ref[b]
        xbuf[b, pad:pad + c, :] = x
        conv = x * cw[CONV_W - 1:CONV_W, :]
        for j in range(1, CONV_W):
            conv = conv + xbuf[b, pad - j:pad - j + c, :] * cw[CONV_W - 1 - j:CONV_W - j, :]
        xbuf[b, 0:pad, :] = x[c - pad:c, :]
        act = _silu(conv)
        ab = ab_ref[b]
        lgb = neg_a * _softplus(ab + dtb)
        beta = _sigmoid(ab)
        g = _mask_dot(tri01, lgb)
        g_t = g.T
        for h in range(n_heads):
            qh = act[:, h * dk:(h + 1) * dk]
            kh = act[:, hd + h * dk:hd + (h + 1) * dk]
            vh = act[:, 2 * hd + h * dk:2 * hd + (h + 1) * dk]
            qh = qh * lax.rsqrt(jnp.sum(qh * qh, axis=-1, keepdims=True) + EPS) * (dk ** -0.5)
            kh = kh * lax.rsqrt(jnp.sum(kh * kh, axis=-1, keepdims=True) + EPS)
            gc = g[:, h:h + 1]
            gr = g_t[h:h + 1, :]
            bcol = beta[:, n_heads + h:n_heads + h + 1]
            rel = jnp.where(incl, jnp.exp(jnp.minimum(gc - gr, 0.0)), 0.0)
            probs.append(dict(b=b, h=h, q=qh, k=kh, v=vh, gc=gc, bcol=bcol, rel=rel, eg=jnp.exp(gc),
                              g_end=g[c - 1:c, h:h + 1]))
    for pr in probs:
        kb16 = pr["k"].astype(BF16)
        pr["a"] = jnp.where(strict, _dot_nt(kb16, kb16) * pr["rel"], 0.0) * pr["bcol"]
        pr["qk"] = _dot_nt(pr["q"], kb16) * pr["rel"]

    eye = jnp.where(r == s, 1.0, 0.0)
    same2 = (r // 2) == (s // 2)
    invs = [eye - jnp.where(same2, pr["a"], 0.0) for pr in probs]
    w = 2
    while w < c:
        lvl_mask = ((r // (2 * w)) == (s // (2 * w))) & ((r // w) != (s // w))
        xs = [_dot_hi(inv, jnp.where(lvl_mask, pr["a"], 0.0)) for inv, pr in zip(invs, probs)]
        invs = [inv - _dot_hi(x, inv) for inv, x in zip(invs, xs)]
        w *= 2
    ws = [_dot_hi(inv, pr["k"] * (pr["bcol"] * pr["eg"])) for inv, pr in zip(invs, probs)]
    us = [_dot_hi(inv, pr["v"] * pr["bcol"]) for inv, pr in zip(invs, probs)]

    for pr, wm, um in zip(probs, ws, us):
        b, h = pr["b"], pr["h"]
        hs = slice(h * dk, (h + 1) * dk)
        st = s_scr[b, h]
        st_bf = st.astype(BF16)
        v_new = um - _dot(wm, st_bf)
        o = _dot(pr["qk"], v_new) + _dot(pr["q"] * pr["eg"], st_bf)
        s_scr[b, h] = st * jnp.exp(pr["g_end"]) + _dot_tn(pr["k"] * jnp.exp(pr["g_end"] - pr["gc"]), v_new)
        o = _rms(o, onorm) * _silu(z_ref[b, :, hs])
        o_ref[b, :, hs] = o.astype(o_ref.dtype)

    @pl.when(c_idx == pl.num_programs(1) - 1)
    def _():
        s_ref[...] = s_scr[...]
        tail_ref[...] = xbuf[:, 0:pad, :]


def _gdn_prompt(qkv, z, ab, conv_w, a_log, dt_bias, onorm, bb):
    bsz, t, n3 = qkv.shape
    hd = z.shape[2]
    dk = hd // H_B
    c = min(CHUNK, t)
    assert t % c == 0 and bsz % bb == 0 and c >= 8
    blk = lambda n: pl.BlockSpec((bb, c, n), lambda i, j: (i, j, 0))
    o, s, tail = pl.pallas_call(
        functools.partial(_gdn_kernel, bb=bb, n_heads=H_B),
        grid=(bsz // bb, t // c),
        in_specs=[blk(n3), blk(hd), blk(LANES), _const_spec(conv_w.shape), _const_spec(a_log.shape),
                  _const_spec(dt_bias.shape), _const_spec(onorm.shape)],
        out_specs=[blk(hd), pl.BlockSpec((bb, H_B, dk, dk), lambda i, j: (i, 0, 0, 0)),
                   pl.BlockSpec((bb, 8, n3), lambda i, j: (i, 0, 0))],
        out_shape=[jax.ShapeDtypeStruct((bsz, t, hd), BF16), jax.ShapeDtypeStruct((bsz, H_B, dk, dk), F32),
                   jax.ShapeDtypeStruct((bsz, 8, n3), F32)],
        scratch_shapes=[pltpu.VMEM((bb, 8 + c, n3), F32), pltpu.VMEM((bb, H_B, dk, dk), F32)],
        compiler_params=_cparams("parallel", "arbitrary"), name="gdn_prompt")(qkv, z, ab, conv_w, a_log, dt_bias, onorm)
    return o, s, tail[:, 8 - (CONV_W - 1):, :]


def _sb_tiles(qm, kblk, vblks, bias2, u01, r_cols, mask):
    z = [_dot_nt(q, kblk) + b for q, b in zip(qm, bias2)]
    sp = []
    for zz in z:
        s2 = jnp.maximum(zz, 0.0) + jnp.log2(1.0 + jnp.exp2(-jnp.abs(zz)))
        sp.append(s2 if mask is None else jnp.where(mask, s2, 0.0))
    inner = [jnp.dot(s2.astype(BF16), u01, preferred_element_type=F32) for s2 in sp]
    pv = None
    r_new = []
    for hh, (zz, s2, inn, r) in enumerate(zip(z, sp, inner, r_cols)):
        tot = s2 + inn
        w = jnp.exp2(zz - (tot + r))
        if mask is not None:
            w = jnp.where(mask, w, 0.0)
        r_new.append(r + tot[:, 0:1])
        d = _dot(w, vblks[hh])
        pv = d if pv is None else pv + d
    return pv, r_new


def _sb_kernel(bias_ref, q_ref, k_ref, v_ref, o_ref, kb, vm, acc, rsum, *, hd, tk):
    p = pl.program_id(1)
    qi = pl.program_id(2)
    tq = q_ref.shape[1]
    t = k_ref.shape[1]
    hpp = LANES // hd
    lane_head_k = lax.broadcasted_iota(jnp.int32, (t, LANES), 1) // hd

    @pl.when(qi == 0)
    def _():
        kb[...] = k_ref[0].astype(BF16)
        v = v_ref[0]
        for hh in range(hpp):
            vm[hh] = jnp.where(lane_head_k == hh, v, 0.0).astype(BF16)

    lane_head_q = lax.broadcasted_iota(jnp.int32, (tq, LANES), 1) // hd
    qs = q_ref[0] * (hd ** -0.5 * LOG2E)
    qm = [jnp.where(lane_head_q == hh, qs, 0.0).astype(BF16) for hh in range(hpp)]
    bias2 = [bias_ref[p * hpp + hh] * LOG2E for hh in range(hpp)]
    rr = lax.broadcasted_iota(jnp.int32, (tk, tk), 0)
    ss = lax.broadcasted_iota(jnp.int32, (tk, tk), 1)
    u01 = jnp.where(rr > ss, 1.0, 0.0).astype(BF16)
    n_sub = tq // tk

    acc[...] = jnp.zeros_like(acc)
    rsum[...] = jnp.zeros_like(rsum)

    def tile(start, mask):
        kblk = kb[pl.ds(start, tk), :]
        vblks = [vm[hh, pl.ds(start, tk), :] for hh in range(hpp)]
        pv, r_new = _sb_tiles(qm, kblk, vblks, bias2, u01, [rsum[hh] for hh in range(hpp)], mask)
        acc[...] += pv
        for hh in range(hpp):
            rsum[hh] = r_new[hh]

    qpos = qi * tq + lax.broadcasted_iota(jnp.int32, (tq, tk), 0)
    for sub in range(n_sub - 1, -1, -1):
        kpos = qi * tq + sub * tk + lax.broadcasted_iota(jnp.int32, (tq, tk), 1)
        tile(pl.multiple_of(qi * tq + sub * tk, tk), kpos < qpos)

    def body(i, carry):
        start = pl.multiple_of((qi * n_sub - 1 - i) * tk, tk)
        tile(start, None)
        return carry

    lax.fori_loop(0, qi * n_sub, body, 0)
    o_ref[0] = acc[...].astype(o_ref.dtype)


def _sb_prompt(q, k, v, bias, tq=512, tk=256):
    bsz, t, hhd = q.shape
    hd = hhd // H_C
    tq, tk = min(tq, t), min(tk, t)
    assert t % tq == 0 and tq % tk == 0 and LANES % hd == 0
    n_pairs = hhd // LANES
    hpp = LANES // hd
    return pl.pallas_call(
        functools.partial(_sb_kernel, hd=hd, tk=tk),
        grid=(bsz, n_pairs, t // tq),
        in_specs=[pl.BlockSpec(memory_space=pltpu.SMEM),
                  pl.BlockSpec((1, tq, LANES), lambda b, p, i: (b, i, p)),
                  pl.BlockSpec((1, t, LANES), lambda b, p, i: (b, 0, p)),
                  pl.BlockSpec((1, t, LANES), lambda b, p, i: (b, 0, p))],
        out_specs=pl.BlockSpec((1, tq, LANES), lambda b, p, i: (b, i, p)),
        out_shape=jax.ShapeDtypeStruct((bsz, t, hhd), BF16),
        scratch_shapes=[pltpu.VMEM((t, LANES), BF16), pltpu.VMEM((hpp, t, LANES), BF16),
                        pltpu.VMEM((tq, LANES), F32), pltpu.VMEM((hpp, tq, 1), F32)],
        compiler_params=_cparams("parallel", "parallel", "arbitrary"), name="sb_prompt")(bias, q, k, v)


def _rg_kernel(x_ref, y_ref, cw_ref, cb_ref, wa_ref, ba_ref, wx_ref, bx_ref, lam_ref,
               o_ref, h_ref, tail_ref, xbuf, a_scr, b_scr, h_scr):
    ti = pl.program_id(1)
    tt = x_ref.shape[1]
    d = x_ref.shape[2]
    pad = 8

    @pl.when(ti == 0)
    def _():
        h_scr[...] = jnp.zeros_like(h_scr)
        xbuf[0:pad, :] = jnp.zeros((pad, d), F32)

    x = x_ref[0]
    cw = cw_ref[...]
    xbuf[pad:pad + tt, :] = x
    conv = x * cw[CONV_W - 1:CONV_W, :]
    for j in range(1, CONV_W):
        conv = conv + xbuf[pad - j:pad - j + tt, :] * cw[CONV_W - 1 - j:CONV_W - j, :]
    xbuf[0:pad, :] = x[tt - pad:tt, :]
    xc = conv + cb_ref[...]
    nsp = _softplus(-lam_ref[...])
    xb = xc.astype(BF16)
    rg = _sigmoid(jnp.dot(xb, wa_ref[...], preferred_element_type=F32) + ba_ref[...])
    ig = _sigmoid(jnp.dot(xb, wx_ref[...], preferred_element_type=F32) + bx_ref[...])
    log_a = (-RG_C) * rg * nsp
    a = jnp.exp(log_a)
    bt = jnp.sqrt(_one_minus_exp2(log_a, a)) * (ig * xc)
    pos = ti * tt + lax.broadcasted_iota(jnp.int32, (tt, d), 0)
    a = jnp.where(pos == 0, 0.0, a)
    bt = jnp.where(pos == 0, ig * xc, bt)

    row8 = lax.broadcasted_iota(jnp.int32, (tt, d), 0) % 8
    for sh in (1, 2, 4):
        a_prev = jnp.where(row8 >= sh, pltpu.roll(a, sh, 0), 1.0)
        b_prev = jnp.where(row8 >= sh, pltpu.roll(bt, sh, 0), 0.0)
        bt = a * b_prev + bt
        a = a * a_prev
    a_scr[...] = a
    b_scr[...] = bt

    def body(i, h):
        r0 = pl.multiple_of(i * 8, 8)
        hg = a_scr[pl.ds(r0, 8), :] * h + b_scr[pl.ds(r0, 8), :]
        b_scr[pl.ds(r0, 8), :] = hg
        return hg[7:8, :]

    h_last = lax.fori_loop(0, tt // 8, body, h_scr[...])
    h_scr[...] = h_last
    hs = b_scr[...]
    o_ref[0] = (hs * _gelu_tanh(y_ref[0])).astype(o_ref.dtype)

    @pl.when(ti == pl.num_programs(1) - 1)
    def _():
        h_ref[0] = h_last
        tail_ref[0] = xbuf[0:pad, :]


def _rg_prompt(xr, yr, conv_w, conv_b, wa_bd, b_a, wx_bd, b_x, lam, tt=512):
    bsz, t, d = xr.shape
    tt = min(tt, t)
    assert t % tt == 0 and tt % 8 == 0
    blk = pl.BlockSpec((1, tt, d), lambda b, i: (b, i, 0))
    consts = (conv_w, conv_b, wa_bd, b_a, wx_bd, b_x, lam)
    o, h, tail = pl.pallas_call(
        _rg_kernel,
        grid=(bsz, t // tt),
        in_specs=[blk, blk] + [_const_spec(a.shape) for a in consts],
        out_specs=[blk, pl.BlockSpec((1, 1, d), lambda b, i: (b, 0, 0)), pl.BlockSpec((1, 8, d), lambda b, i: (b, 0, 0))],
        out_shape=[jax.ShapeDtypeStruct((bsz, t, d), BF16), jax.ShapeDtypeStruct((bsz, 1, d), F32),
                   jax.ShapeDtypeStruct((bsz, 8, d), F32)],
        scratch_shapes=[pltpu.VMEM((8 + tt, d), F32), pltpu.VMEM((tt, d), F32), pltpu.VMEM((tt, d), F32),
                        pltpu.VMEM((1, d), F32)],
        compiler_params=_cparams("parallel", "arbitrary"), name="rglru_prompt")(xr, yr, *consts)
    return o, h[:, 0, :], tail[:, 8 - (CONV_W - 1):, :]


def _even_prep_kernel(x_ref, hist_ref, cw_ref, ab_ref, alog_ref, dtb_ref, lgp_ref,
                      q_ref, k_ref, v_ref, nh_ref, gb_ref, lg_ref, *, n_heads):
    x = x_ref[...]
    cw = cw_ref[...]
    conv = x * cw[CONV_W - 1:CONV_W, :]
    for j in range(CONV_W - 1):
        conv = conv + hist_ref[j] * cw[j:j + 1, :]
    for j in range(CONV_W - 2):
        nh_ref[j] = hist_ref[j + 1]
    nh_ref[CONV_W - 2] = x
    act = _silu(conv)
    hd = q_ref.shape[1]
    dk = hd // n_heads
    for h in range(n_heads):
        hs = slice(h * dk, (h + 1) * dk)
        qh = act[:, hs]
        kh = act[:, hd + h * dk:hd + (h + 1) * dk]
        q_ref[:, hs] = qh * lax.rsqrt(jnp.sum(qh * qh, axis=-1, keepdims=True) + EPS) * (dk ** -0.5)
        k_ref[:, hs] = kh * lax.rsqrt(jnp.sum(kh * kh, axis=-1, keepdims=True) + EPS)
    v_ref[...] = act[:, 2 * hd:3 * hd]
    ab = ab_ref[...]
    lane = lax.broadcasted_iota(jnp.int32, ab.shape, 1)
    lgb = -jnp.exp(alog_ref[...]) * _softplus(ab + dtb_ref[...])
    gb_ref[...] = jnp.where(lane < n_heads, lgb, _sigmoid(ab))
    lg_ref[...] = -_softplus(-lgp_ref[...]) * (1.0 / GLA_TAU)


def _even_state_kernel(qa_ref, ka_ref, lga_ref, va_ref, qb_ref, kb_ref, vb_ref, gb_ref, sa_ref, sb_ref,
                       oa_ref, ob_ref, sa_out, sb_out, *, nb, dk_a, dv_a, dk_b):
    for j in range(nb):
        gbrow = gb_ref[j:j + 1, :]
        for h in range(H_A):
            rs = slice(h * dk_a, (h + 1) * dk_a)
            s_new = sa_ref[j, h] * jnp.exp(lga_ref[0, rs, j:j + 1]) + ka_ref[0, rs, j:j + 1] * va_ref[j:j + 1, h * dv_a:(h + 1) * dv_a]
            sa_out[j, h] = s_new
            oa_ref[j:j + 1, h * dv_a:(h + 1) * dv_a] = jnp.sum(qa_ref[0, rs, j:j + 1] * (dk_a ** -0.5) * s_new, axis=0, keepdims=True)
        for h in range(H_B):
            rs = slice(h * dk_b, (h + 1) * dk_b)
            st = sb_ref[j, h]
            eg = jnp.exp(gbrow[:, h:h + 1])
            beta = gbrow[:, H_B + h:H_B + h + 1]
            kcol = kb_ref[0, rs, j:j + 1]
            vrow = vb_ref[j:j + 1, rs]
            v_new = vrow * beta - jnp.sum((kcol * (beta * eg)) * st, axis=0, keepdims=True)
            s_new = st * eg + kcol * v_new
            sb_out[j, h] = s_new
            ob_ref[j:j + 1, rs] = jnp.sum(qb_ref[0, rs, j:j + 1] * s_new, axis=0, keepdims=True)


def _even_out_kernel(oa_ref, ob_ref, r_ref, z_ref, ona_ref, onb_ref, o_ref, *, dv_a, dv_b):
    n_a = oa_ref.shape[1]
    for h in range(n_a // dv_a):
        sl = slice(h * dv_a, (h + 1) * dv_a)
        o_ref[:, sl] = (_rms(oa_ref[:, sl], ona_ref[...]) * _silu(r_ref[:, sl])).astype(o_ref.dtype)
    for h in range(ob_ref.shape[1] // dv_b):
        sl = slice(h * dv_b, (h + 1) * dv_b)
        o_ref[:, n_a + h * dv_b:n_a + (h + 1) * dv_b] = (_rms(ob_ref[:, sl], onb_ref[...]) * _silu(z_ref[:, sl])).astype(o_ref.dtype)


def _cols(x, nb):
    n, d = x.shape
    return x.reshape(n // nb, nb, d).transpose(0, 2, 1)


def _even_sample(qa, ka, va, ra, lgp, qkv, zb, ab, s_gla, s_gdn, hist, conv_w, a_log, dt_bias, on_a, on_b, nb=8):
    n, hdk_a = qa.shape
    hd_b = zb.shape[1]
    n3 = qkv.shape[1]
    hist_t = hist.transpose(1, 0, 2)
    full = lambda a: pl.BlockSpec(a.shape, lambda *_: (0,) * a.ndim)
    sds = lambda *s: jax.ShapeDtypeStruct(s, F32)
    ins = (qkv, hist_t, conv_w, ab, a_log, dt_bias, lgp)
    outs = [sds(n, hd_b), sds(n, hd_b), sds(n, hd_b), sds(CONV_W - 1, n, n3), sds(n, LANES), sds(n, hdk_a)]
    qb, kb, vb, new_hist_t, gb, lga = pl.pallas_call(
        functools.partial(_even_prep_kernel, n_heads=H_B),
        in_specs=[full(a) for a in ins], out_specs=[full(o) for o in outs], out_shape=outs,
        compiler_params=pltpu.CompilerParams(vmem_limit_bytes=VMEM_LIMIT_BYTES), name="even_sample_prep")(*ins)

    dk_a, dv_a, dk_b = hdk_a // H_A, va.shape[1] // H_A, hd_b // H_B
    assert n % nb == 0
    colspec = lambda d: pl.BlockSpec((1, d, nb), lambda i: (i, 0, 0))
    rowspec = lambda d: pl.BlockSpec((nb, d), lambda i: (i, 0))
    sa_spec = pl.BlockSpec((nb, H_A, dk_a, dv_a), lambda i: (i, 0, 0, 0))
    sb_spec = pl.BlockSpec((nb, H_B, dk_b, dk_b), lambda i: (i, 0, 0, 0))
    oa, ob, s_gla_new, s_gdn_new = pl.pallas_call(
        functools.partial(_even_state_kernel, nb=nb, dk_a=dk_a, dv_a=dv_a, dk_b=dk_b),
        grid=(n // nb,),
        in_specs=[colspec(hdk_a), colspec(hdk_a), colspec(hdk_a), rowspec(va.shape[1]),
                  colspec(hd_b), colspec(hd_b), rowspec(hd_b), rowspec(LANES), sa_spec, sb_spec],
        out_specs=[rowspec(va.shape[1]), rowspec(hd_b), sa_spec, sb_spec],
        out_shape=[sds(n, va.shape[1]), sds(n, hd_b), sds(*s_gla.shape), sds(*s_gdn.shape)],
        compiler_params=_cparams("parallel"), name="even_sample_state")(
            _cols(qa, nb), _cols(ka, nb), _cols(lga, nb), va, _cols(qb, nb), _cols(kb, nb), vb, gb, s_gla, s_gdn)

    ins = (oa, ob, ra, zb, on_a, on_b)
    out = jax.ShapeDtypeStruct((n, va.shape[1] + hd_b), BF16)
    o = pl.pallas_call(
        functools.partial(_even_out_kernel, dv_a=dv_a, dv_b=dk_b),
        in_specs=[full(a) for a in ins], out_specs=full(out), out_shape=out,
        compiler_params=pltpu.CompilerParams(vmem_limit_bytes=VMEM_LIMIT_BYTES), name="even_sample_out")(*ins)
    return o, s_gla_new, s_gdn_new, new_hist_t.transpose(1, 0, 2)


def _sb_decode_kernel(pt_ref, bias_ref, q_ref, sel_ref, selt_ref, *rest, pps):
    k_refs = rest[:pps]
    v_refs = rest[pps:2 * pps]
    o_ref, acc_scr, r_scr = rest[2 * pps:]
    grp = pl.program_id(1)
    page = k_refs[0].shape[1]
    n_heads, hd = q_ref.shape[1], q_ref.shape[2]
    flat = page * n_heads

    @pl.when(grp == 0)
    def _():
        acc_scr[...] = jnp.zeros_like(acc_scr)
        r_scr[...] = jnp.zeros_like(r_scr)

    q = (q_ref[0] * (hd ** -0.5 * LOG2E)).astype(BF16)
    own = (lax.broadcasted_iota(jnp.int32, (pps * n_heads, flat), 0) % n_heads
           == lax.broadcasted_iota(jnp.int32, (pps * n_heads, flat), 1) % n_heads)
    rr = lax.broadcasted_iota(jnp.int32, (page, page), 0)
    ss = lax.broadcasted_iota(jnp.int32, (page, page), 1)
    u01 = jnp.where(rr > ss, 1.0, 0.0).astype(BF16)
    bias2 = jnp.concatenate([bias_ref[...] * LOG2E] * pps, axis=0)

    zf = jnp.concatenate([_dot_nt(q, k_refs[j][0].reshape(flat, hd)) for j in range(pps)], axis=0)
    zd_hi, zd_lo = _split2(jnp.where(own, zf, 0.0))
    sel = sel_ref[...]
    z_all = (jnp.dot(zd_hi, sel, preferred_element_type=F32) + jnp.dot(zd_lo, sel, preferred_element_type=F32)) + bias2
    sp = jnp.maximum(z_all, 0.0) + jnp.log2(1.0 + jnp.exp2(-jnp.abs(z_all)))
    tot = sp + jnp.dot(sp.astype(BF16), u01, preferred_element_type=F32)
    r_col = r_scr[...]
    ws = []
    for j in range(pps):
        rows = slice(j * n_heads, (j + 1) * n_heads)
        ws.append(jnp.exp2(z_all[rows] - (tot[rows] + r_col)))
        r_col = r_col + tot[rows, 0:1]
    r_scr[...] = r_col
    w_all = jnp.concatenate(ws, axis=0).astype(BF16)
    wf = jnp.where(own, jnp.dot(w_all, selt_ref[...], preferred_element_type=F32), 0.0).astype(BF16)
    acc = acc_scr[...]
    for j in range(pps):
        acc = acc + _dot(wf[j * n_heads:(j + 1) * n_heads], v_refs[j][0].reshape(flat, hd))
    acc_scr[...] = acc

    @pl.when(grp == pl.num_programs(1) - 1)
    def _():
        o_ref[0] = acc


def _sb_decode(q, cache_k, cache_v, page_table, bias, pps=8):
    n, hhd = q.shape
    n_pool, page, n_heads, hd = cache_k.shape
    n_pages = page_table.shape[1]
    pps = min(pps, n_pages)
    assert n_pages % pps == 0
    pt = page_table.reshape(-1)
    flat = page * n_heads
    sel = (jnp.arange(flat)[:, None] // n_heads == jnp.arange(page)[None, :]).astype(BF16)

    def page_spec(j):
        return pl.BlockSpec((1, page, n_heads, hd),
                            lambda b, g, pt_ref: (pt_ref[b * n_pages + n_pages - 1 - (g * pps + j)], 0, 0, 0))

    const = lambda shape: pl.BlockSpec(shape, lambda b, g, pt_ref: (0,) * len(shape))
    grid_spec = pltpu.PrefetchScalarGridSpec(
        num_scalar_prefetch=1, grid=(n, n_pages // pps),
        in_specs=[const((n_heads, 1)), pl.BlockSpec((1, n_heads, hd), lambda b, g, pt_ref: (b, 0, 0)),
                  const((flat, page)), const((page, flat))]
                 + [page_spec(j) for j in range(pps)] * 2,
        out_specs=pl.BlockSpec((1, n_heads, hd), lambda b, g, pt_ref: (b, 0, 0)),
        scratch_shapes=[pltpu.VMEM((n_heads, hd), F32), pltpu.VMEM((n_heads, 1), F32)])
    o = pl.pallas_call(
        functools.partial(_sb_decode_kernel, pps=pps),
        grid_spec=grid_spec, out_shape=jax.ShapeDtypeStruct((n, n_heads, hd), F32),
        compiler_params=_cparams("parallel", "arbitrary"), name="sb_decode")(
            pt, bias.reshape(n_heads, 1), q.reshape(n, n_heads, hd), sel, sel.T,
            *([cache_k] * pps), *([cache_v] * pps))
    return o.reshape(n, hhd)


def _rg_decode_kernel(x_ref, y_ref, oc_ref, hist_ref, h0_ref, cw_ref, cb_ref, wa_ref, ba_ref, wx_ref, bx_ref, lam_ref,
                      o_ref, h_ref, nh_ref, *, reset):
    x = x_ref[...]
    cw = cw_ref[...]
    conv = x * cw[CONV_W - 1:CONV_W, :]
    for j in range(CONV_W - 1):
        conv = conv + hist_ref[j] * cw[j:j + 1, :]
    for j in range(CONV_W - 2):
        nh_ref[j] = hist_ref[j + 1]
    nh_ref[CONV_W - 2] = x
    xc = conv + cb_ref[...]
    xb = xc.astype(BF16)
    rg = _sigmoid(jnp.dot(xb, wa_ref[...], preferred_element_type=F32) + ba_ref[...])
    ig = _sigmoid(jnp.dot(xb, wx_ref[...], preferred_element_type=F32) + bx_ref[...])
    log_a = (-RG_C) * rg * _softplus(-lam_ref[...])
    if reset:
        h = ig * xc
    else:
        a = jnp.exp(log_a)
        h = a * h0_ref[...] + jnp.sqrt(_one_minus_exp2(log_a, a)) * (ig * xc)
    h_ref[...] = h
    d_c = oc_ref.shape[1]
    o_ref[:, 0:d_c] = oc_ref[...].astype(o_ref.dtype)
    o_ref[:, d_c:] = (h * _gelu_tanh(y_ref[...])).astype(o_ref.dtype)


def _odd_sample(oc, xr, yr, hist, h0, conv_w, conv_b, wa_bd, b_a, wx_bd, b_x, lam, offset):
    n, d = xr.shape
    hist_t = hist.transpose(1, 0, 2)
    full = lambda a: pl.BlockSpec(a.shape, lambda *_: (0,) * a.ndim)
    ins = (xr, yr, oc, hist_t, h0, conv_w, conv_b, wa_bd, b_a, wx_bd, b_x, lam)
    outs = [jax.ShapeDtypeStruct((n, oc.shape[1] + d), BF16), jax.ShapeDtypeStruct((n, d), F32),
            jax.ShapeDtypeStruct((CONV_W - 1, n, d), F32)]
    o, h, nh = pl.pallas_call(
        functools.partial(_rg_decode_kernel, reset=(offset == 0)),
        in_specs=[full(a) for a in ins], out_specs=[full(a) for a in outs], out_shape=outs,
        compiler_params=pltpu.CompilerParams(vmem_limit_bytes=VMEM_LIMIT_BYTES), name="rglru_decode")(*ins)
    return o, h, nh.transpose(1, 0, 2)


def _pad_cols(w, n):
    return jnp.pad(w, ((0, 0), (0, n - w.shape[1])))


def _block_diag(w):
    h, a, b = w.shape
    eye = jnp.eye(h, dtype=w.dtype)
    return (eye[:, None, :, None] * w[:, :, None, :]).reshape(h * a, h * b)


def _row(v, n=None):
    v = v.reshape(1, -1).astype(F32)
    return v if n is None else _pad_cols(v, n)


def kernel(x_prompt, x_sample, state_gla, state_gdn, state_gdn_conv, cache_sb_k, cache_sb_v, state_rg_h, state_rg_conv, page_table, ln_ffn, ffn_w_gate, ffn_w_up, ffn_w_down, ln_mix_e, w_in_e, gla_w_gate, gla_b_gate, gla_onorm, gdn_conv_w, gdn_a_log, gdn_dt_bias, gdn_onorm, w_out_e, ln_mix_o, w_in_o, sb_bias, rg_conv_w, rg_conv_b, rg_w_a, rg_b_a, rg_w_x, rg_b_x, rg_lambda, w_out_o, ln_final):
    bp, t, d = x_prompt.shape
    bs, ts, _ = x_sample.shape
    assert ts == 1
    depth = ln_ffn.shape[0]
    dk_a, dv_a = state_gla.shape[3], state_gla.shape[4]
    dk_b = state_gdn.shape[3]
    hd_c = cache_sb_k.shape[4]
    d_rnn = state_rg_h.shape[2]
    rank = gla_w_gate.shape[1]
    n_qa, n_va, n_b, n_c = H_A * dk_a, H_A * dv_a, H_B * dk_b, H_C * hd_c
    offset = page_table.shape[1] * cache_sb_k.shape[2]

    xp = x_prompt.reshape(bp * t, d)
    xs = x_sample.reshape(bs, d)
    outs = {}
    mix_p = mix_s = None
    for layer in range(depth):
        ffn = [(_row(ln_ffn[layer, i]), ffn_w_gate[layer, i].astype(BF16), ffn_w_up[layer, i].astype(BF16),
                ffn_w_down[layer, i].astype(BF16)) for i in range(2)]
        if layer % 2 == 0:
            e = layer // 2
            w = w_in_e[e]
            c0 = 0
            cols = []
            for n in (n_qa, n_qa, n_va, n_va, rank, 3 * n_b, n_b, 2 * H_B):
                cols.append(w[:, c0:c0 + n])
                c0 += n
            w_q, w_k, w_v, w_r, w_la, w_qkv, w_z, w_ab = cols
            w_list = [a.astype(BF16) for a in (w_q, w_k, w_v, w_r, w_qkv, w_z, _pad_cols(w_ab, LANES), _pad_cols(w_la, LANES))]
            gate = (jnp.pad(gla_w_gate[e], ((0, LANES - rank), (0, 0))).astype(BF16), _row(gla_b_gate[e]))
            proj = (_row(ln_mix_e[e]), w_list)
            w_out = w_out_e[e].astype(BF16)
        else:
            o = layer // 2
            w = w_in_o[o]
            w_list = [w[:, i * n_c:(i + 1) * n_c].astype(BF16) for i in range(3)]
            w_list += [w[:, 3 * n_c:3 * n_c + d_rnn].astype(BF16), w[:, 3 * n_c + d_rnn:].astype(BF16)]
            gate = None
            proj = (_row(ln_mix_o[o]), w_list)
            w_out = w_out_o[o].astype(BF16)

        xp, *pp = _block_call(xp, None, ffn[0], proj=proj, gate=gate)
        xs, *ps = _block_call(xs, None, ffn[0], proj=proj, gate=gate)

        if layer % 2 == 0:
            e = layer // 2
            r3 = lambda a: a.reshape(bp, t, a.shape[1])
            qa, ka, va, ra, qkv, zb, ab, lgp = pp
            on_a, on_b = _row(gla_onorm[e]), _row(gdn_onorm[e])
            a_log, dtb = _row(gdn_a_log[e], LANES), _row(gdn_dt_bias[e], LANES)
            oa, s_gla_p = _gla_prompt(r3(qa), r3(ka), r3(va), r3(ra), r3(lgp), on_a, bb=bp)
            ob, s_gdn_p, tail_p = _gdn_prompt(r3(qkv), r3(zb), r3(ab), gdn_conv_w[e], a_log, dtb, on_b, bb=bp)
            mix_p = ([oa.reshape(bp * t, n_va), ob.reshape(bp * t, n_b)], [w_out[:n_va], w_out[n_va:]])
            qa, ka, va, ra, qkv, zb, ab, lgp = ps
            o_s, s_gla_s, s_gdn_s, hist_s = _even_sample(qa, ka, va, ra, lgp, qkv, zb, ab, state_gla[e], state_gdn[e],
                                                         state_gdn_conv[e], gdn_conv_w[e], a_log, dtb, on_a, on_b)
            mix_s = ([o_s], [w_out])
            outs.setdefault("gla_p", []).append(s_gla_p)
            outs.setdefault("gla_s", []).append(s_gla_s)
            outs.setdefault("gdn_p", []).append(s_gdn_p)
            outs.setdefault("gdn_s", []).append(s_gdn_s)
            outs.setdefault("gcv_p", []).append(tail_p)
            outs.setdefault("gcv_s", []).append(hist_s)
        else:
            o = layer // 2
            r3 = lambda a: a.reshape(bp, t, a.shape[1])
            qc, kc, vc, xr, yr = pp
            wa_bd, wx_bd = _block_diag(rg_w_a[o]).astype(BF16), _block_diag(rg_w_x[o]).astype(BF16)
            rg_consts = (rg_conv_w[o], _row(rg_conv_b[o]), wa_bd, _row(rg_b_a[o]), wx_bd, _row(rg_b_x[o]), _row(rg_lambda[o]))
            oc = _sb_prompt(r3(qc), r3(kc), r3(vc), sb_bias[o])
            od, h_p, tail_p = _rg_prompt(r3(xr), r3(yr), *rg_consts)
            mix_p = ([oc.reshape(bp * t, n_c), od.reshape(bp * t, d_rnn)], [w_out[:n_c], w_out[n_c:]])
            outs.setdefault("sbk_p", []).append(kc.reshape(bp, t, H_C, hd_c))
            outs.setdefault("sbv_p", []).append(vc.reshape(bp, t, H_C, hd_c))
            outs.setdefault("rgh_p", []).append(h_p)
            outs.setdefault("rgc_p", []).append(tail_p)
            qc, kc, vc, xr, yr = ps
            oc_s = _sb_decode(qc, cache_sb_k[o], cache_sb_v[o], page_table, sb_bias[o])
            o_s, h_s, hist_s = _odd_sample(oc_s, xr, yr, state_rg_conv[o], state_rg_h[o], *rg_consts, offset)
            mix_s = ([o_s], [w_out])
            outs.setdefault("sbk_s", []).append(kc.reshape(bs, 1, H_C, hd_c))
            outs.setdefault("sbv_s", []).append(vc.reshape(bs, 1, H_C, hd_c))
            outs.setdefault("rgh_s", []).append(h_s)
            outs.setdefault("rgc_s", []).append(hist_s)

        final_g = _row(ln_final) if layer == depth - 1 else None
        (xp,) = _block_call(xp, mix_p, ffn[1], final_g=final_g)
        (xs,) = _block_call(xs, mix_s, ffn[1], final_g=final_g)

    st = lambda name: jnp.stack(outs[name])
    return (xp.reshape(bp, t, d), xs.reshape(bs, 1, d), st("gla_p"), st("gla_s"), st("gdn_p"), st("gdn_s"),
            st("gcv_p"), st("gcv_s"), st("sbk_p"), st("sbk_s"), st("sbv_p"), st("sbv_s"),
            st("rgh_p"), st("rgh_s"), st("rgc_p"), st("rgc_s"))
```

```python
import functools
import math

import jax
import jax.numpy as jnp
from jax import lax
from jax.experimental import pallas as pl
from jax.experimental.pallas import tpu as pltpu

F32 = jnp.float32
BF16 = jnp.bfloat16

H_A = 4
GLA_TAU = 16.0
H_B = 4
CONV_W = 4
H_C = 8
H_D = 8
RG_C = 8.0
CHUNK = 64
EPS = 1e-6
LOG2E = 1.4426950408889634

VMEM_LIMIT_BYTES = 56 * 1024 * 1024
LANES = 128


def _cparams(*sem):
    return pltpu.CompilerParams(dimension_semantics=sem, vmem_limit_bytes=VMEM_LIMIT_BYTES)


def _dot(a, b):
    return jnp.dot(a.astype(BF16), b.astype(BF16), preferred_element_type=F32)


def _dot_nt(a, b):
    return lax.dot_general(a.astype(BF16), b.astype(BF16), (((1,), (1,)), ((), ())), preferred_element_type=F32)


def _dot_tn(a, b):
    return lax.dot_general(a.astype(BF16), b.astype(BF16), (((0,), (0,)), ((), ())), preferred_element_type=F32)


def _split2(x):
    hi = x.astype(BF16)
    lo = (x - hi.astype(F32)).astype(BF16)
    return hi, lo


def _split3(x):
    x1 = x.astype(BF16)
    r1 = x - x1.astype(F32)
    x2 = r1.astype(BF16)
    x3 = (r1 - x2.astype(F32)).astype(BF16)
    return x1, x2, x3


def _dot_hi(a, b):
    a1, a2 = _split2(a)
    b1, b2 = _split2(b)
    d = lambda u, v: jnp.dot(u, v, preferred_element_type=F32)
    return d(a1, b1) + (d(a1, b2) + d(a2, b1))


def _mask_dot(m01, x):
    x1, x2, x3 = _split3(x)
    d = lambda v: jnp.dot(m01, v, preferred_element_type=F32)
    return d(x1) + (d(x2) + d(x3))


def _rms(x, g):
    return x * lax.rsqrt(jnp.mean(x * x, axis=-1, keepdims=True) + EPS) * g


def _sigmoid(x):
    return 1.0 / (1.0 + jnp.exp(-x))


def _silu(x):
    return x * _sigmoid(x)


def _softplus(x):
    return jnp.maximum(x, 0.0) + jnp.log1p(jnp.exp(-jnp.abs(x)))


def _gelu_tanh(x):
    return 0.5 * x * (1.0 + jnp.tanh(math.sqrt(2.0 / math.pi) * (x + 0.044715 * (x * x * x))))


def _one_minus_exp2(log_a, a):
    return jnp.tanh(-log_a) * (a * a + 1.0)


def _const_spec(shape):
    nd = len(shape)
    return pl.BlockSpec(shape, lambda *_: (0,) * nd, pipeline_mode=pl.Buffered(1))


def _block_kernel(*refs, n_mix, n_proj, n_tproj, has_gate, has_final, ff_chunk):
    it = iter(refs)
    x_ref = next(it)
    o_refs = [next(it) for _ in range(n_mix)]
    wo_refs = [next(it) for _ in range(n_mix)]
    gf_ref, wg_ref, wu_ref, wd_ref = next(it), next(it), next(it), next(it)
    if has_final:
        gfin_ref = next(it)
    if n_proj:
        gm_ref = next(it)
        wp_refs = [next(it) for _ in range(n_proj)]
    if has_gate:
        wgate_ref, bgate_ref = next(it), next(it)
    wt_refs = [next(it) for _ in range(n_tproj)]
    xo_ref = next(it)
    po_refs = [next(it) for _ in range(n_proj)]
    tp_refs = [next(it) for _ in range(n_tproj)]

    x = x_ref[...]
    for o_ref, wo_ref in zip(o_refs, wo_refs):
        x = x + _dot(o_ref[...], wo_ref[...])
    h = _rms(x, gf_ref[...]).astype(BF16)
    d_ff = wg_ref.shape[1]
    acc = jnp.zeros_like(x)
    for c0 in range(0, d_ff, ff_chunk):
        gt = jnp.dot(h, wg_ref[:, c0:c0 + ff_chunk], preferred_element_type=F32)
        up = jnp.dot(h, wu_ref[:, c0:c0 + ff_chunk], preferred_element_type=F32)
        act = (_silu(gt) * up).astype(BF16)
        acc = acc + jnp.dot(act, wd_ref[c0:c0 + ff_chunk, :], preferred_element_type=F32)
    x = x + 0.5 * acc
    if has_final:
        xo_ref[...] = _rms(x, gfin_ref[...])
    else:
        xo_ref[...] = x
    if n_proj:
        hm = _rms(x, gm_ref[...]).astype(BF16)
        for j, (wp_ref, po_ref) in enumerate(zip(wp_refs, po_refs)):
            p = jnp.dot(hm, wp_ref[...], preferred_element_type=F32)
            if has_gate and j == n_proj - 1:
                p = _dot(p, wgate_ref[...]) + bgate_ref[...]
            po_ref[...] = p
        for wt_ref, tp_ref in zip(wt_refs, tp_refs):
            tp_ref[0] = lax.dot_general(wt_ref[...], hm, (((1,), (1,)), ((), ())), preferred_element_type=F32)


def _block_call(x, mix, ffn, final_g=None, proj=None, gate=None, tproj=None, tm=256):
    m, d = x.shape
    tm = min(tm, m)
    assert m % tm == 0
    g_ffn, wg, wu, wd = ffn
    d_ff = wg.shape[1]
    ff_chunk = d_ff // 2 if (d_ff // 2) % LANES == 0 else d_ff
    row = lambda n: pl.BlockSpec((tm, n), lambda i: (i, 0))
    args, specs = [x], [row(d)]
    o_list, wo_list = mix if mix is not None else ((), ())
    for o in o_list:
        args.append(o)
        specs.append(row(o.shape[1]))
    for w in wo_list:
        args.append(w)
        specs.append(_const_spec(w.shape))
    for a in (g_ffn, wg, wu, wd):
        args.append(a)
        specs.append(_const_spec(a.shape))
    if final_g is not None:
        args.append(final_g)
        specs.append(_const_spec(final_g.shape))
    out_shapes, out_specs = [jax.ShapeDtypeStruct((m, d), F32)], [row(d)]
    n_proj = 0
    if proj is not None:
        g_mix, w_list = proj
        n_proj = len(w_list)
        args.append(g_mix)
        specs.append(_const_spec(g_mix.shape))
        for w in w_list:
            args.append(w)
            specs.append(_const_spec(w.shape))
        widths = [w.shape[1] for w in w_list]
        if gate is not None:
            for a in gate:
                args.append(a)
                specs.append(_const_spec(a.shape))
            widths[-1] = gate[0].shape[1]
        for n in widths:
            out_shapes.append(jax.ShapeDtypeStruct((m, n), F32))
            out_specs.append(row(n))
    n_tproj = 0
    if tproj is not None:
        assert proj is not None
        wt_list, bsz = tproj
        n_tproj = len(wt_list)
        t_rows = m // bsz
        assert t_rows % tm == 0
        nt = t_rows // tm
        for w in wt_list:
            args.append(w)
            specs.append(_const_spec(w.shape))
            out_shapes.append(jax.ShapeDtypeStruct((bsz, w.shape[0], t_rows), F32))
            out_specs.append(pl.BlockSpec((1, w.shape[0], tm), lambda i: (i // nt, 0, i % nt)))
    kern = functools.partial(_block_kernel, n_mix=len(o_list), n_proj=n_proj, n_tproj=n_tproj,
                             has_gate=gate is not None, has_final=final_g is not None, ff_chunk=ff_chunk)
    return pl.pallas_call(
        kern, grid=(m // tm,), in_specs=specs, out_specs=out_specs, out_shape=out_shapes,
        compiler_params=_cparams("parallel"), name="block")(*args)


def _tri_incl(c):
    r = lax.broadcasted_iota(jnp.int32, (c, c), 0)
    s = lax.broadcasted_iota(jnp.int32, (c, c), 1)
    return r >= s


def _gla_kernel(q_ref, k_ref, v_ref, r_ref, lg_ref, on_ref, o_ref, st_ref, s_scr, *, bb, dk, dv):
    c_idx = pl.program_id(1)
    c = q_ref.shape[1]
    n_pairs = q_ref.shape[2] // LANES
    hpp = LANES // dk

    @pl.when(c_idx == 0)
    def _():
        s_scr[...] = jnp.zeros_like(s_scr)

    incl = _tri_incl(c)
    tri01 = jnp.where(incl, 1.0, 0.0).astype(BF16)
    lane_head = lax.broadcasted_iota(jnp.int32, (c, LANES), 1) // dk
    onorm = on_ref[...]
    gs = [_mask_dot(tri01, -_softplus(-lg_ref[b]) * (1.0 / GLA_TAU)) for b in range(bb)]
    seqs = []
    for b, g in enumerate(gs):
        g_mid = g[c // 2 - 1:c // 2, :]
        g_end = g[c - 1:c, :]
        qs = q_ref[b] * (dk ** -0.5)
        kk = k_ref[b]
        seqs.append(dict(g_end=g_end,
                         q_state=qs * jnp.exp(g),
                         q_in=qs * jnp.exp(g - g_mid),
                         k_in=(kk * jnp.exp(g_mid - g)).astype(BF16),
                         k_end=kk * jnp.exp(g_end - g)))
    probs = []
    for b, sq in enumerate(seqs):
        for p in range(n_pairs):
            sl = slice(p * LANES, (p + 1) * LANES)
            for hh in range(hpp):
                sel = lane_head == hh
                att = _dot_nt(jnp.where(sel, sq["q_in"][:, sl], 0.0), sq["k_in"][:, sl])
                probs.append(dict(b=b, p=p, hh=hh, sl=sl, sel=sel, att=jnp.where(incl, att, 0.0).astype(BF16)))
    for pr in probs:
        b, p, hh, sl = pr["b"], pr["p"], pr["hh"], pr["sl"]
        vs = slice((p * hpp + hh) * dv, (p * hpp + hh + 1) * dv)
        vh = v_ref[b, :, vs].astype(BF16)
        o = _dot(pr["att"], vh) + _dot_nt(jnp.where(pr["sel"], seqs[b]["q_state"][:, sl], 0.0), s_scr[b, p])
        o = _rms(o, onorm) * _silu(r_ref[b, :, vs])
        o_ref[b, :, vs] = o.astype(o_ref.dtype)
        pr["upd"] = _dot_tn(vh, jnp.where(pr["sel"], seqs[b]["k_end"][:, sl], 0.0))
    for b in range(bb):
        for p in range(n_pairs):
            upd = [pr["upd"] for pr in probs if pr["b"] == b and pr["p"] == p]
            sl = slice(p * LANES, (p + 1) * LANES)
            s_scr[b, p] = s_scr[b, p] * jnp.exp(seqs[b]["g_end"][:, sl]) + sum(upd[1:], upd[0])

    @pl.when(c_idx == pl.num_programs(1) - 1)
    def _():
        st_ref[...] = s_scr[...]


def _gla_prompt(q, k, v, r, lg, onorm, bb):
    bsz, t, hdk = q.shape
    hdv = v.shape[2]
    dk, dv = hdk // H_A, hdv // H_A
    c = min(CHUNK, t)
    assert t % c == 0 and bsz % bb == 0 and LANES % dk == 0
    n_pairs = hdk // LANES
    blk = lambda n: pl.BlockSpec((bb, c, n), lambda i, j: (i, j, 0))
    o, st = pl.pallas_call(
        functools.partial(_gla_kernel, bb=bb, dk=dk, dv=dv),
        grid=(bsz // bb, t // c),
        in_specs=[blk(hdk), blk(hdk), blk(hdv), blk(hdv), blk(hdk), _const_spec(onorm.shape)],
        out_specs=[blk(hdv), pl.BlockSpec((bb, n_pairs, dv, LANES), lambda i, j: (i, 0, 0, 0))],
        out_shape=[jax.ShapeDtypeStruct((bsz, t, hdv), BF16), jax.ShapeDtypeStruct((bsz, n_pairs, dv, LANES), F32)],
        scratch_shapes=[pltpu.VMEM((bb, n_pairs, dv, LANES), F32)],
        compiler_params=_cparams("parallel", "arbitrary"), name="gla_prompt")(q, k, v, r, lg, onorm)
    hpp = LANES // dk
    s = st.reshape(bsz, n_pairs, dv, hpp, dk).transpose(0, 1, 3, 4, 2).reshape(bsz, H_A, dk, dv)
    return o, s


def _gdn_kernel(x_ref, z_ref, ab_ref, cw_ref, alog_ref, dtb_ref, on_ref, o_ref, s_ref, tail_ref, xbuf, s_scr,
                *, bb, n_heads):
    c_idx = pl.program_id(1)
    c = x_ref.shape[1]
    hd = z_ref.shape[2]
    dk = hd // n_heads
    pad = 8

    @pl.when(c_idx == 0)
    def _():
        s_scr[...] = jnp.zeros_like(s_scr)
        xbuf[:, 0:pad, :] = jnp.zeros((bb, pad, xbuf.shape[2]), F32)

    incl = _tri_incl(c)
    r = lax.broadcasted_iota(jnp.int32, (c, c), 0)
    s = lax.broadcasted_iota(jnp.int32, (c, c), 1)
    strict = r > s
    tri01 = jnp.where(incl, 1.0, 0.0).astype(BF16)
    cw = cw_ref[...]
    onorm = on_ref[...]
    neg_a = -jnp.exp(alog_ref[...])
    dtb = dtb_ref[...]
    probs = []
    for b in range(bb):
        x = x_ref[b]
        xbuf[b, pad:pad + c, :] = x
        conv = x * cw[CONV_W - 1:CONV_W, :]
        for j in range(1, CONV_W):
            conv = conv + xbuf[b, pad - j:pad - j + c, :] * cw[CONV_W - 1 - j:CONV_W - j, :]
        xbuf[b, 0:pad, :] = x[c - pad:c, :]
        act = _silu(conv)
        ab = ab_ref[b]
        lgb = neg_a * _softplus(ab + dtb)
        beta = _sigmoid(ab)
        g = _mask_dot(tri01, lgb)
        g_t = g.T
        for h in range(n_heads):
            qh = act[:, h * dk:(h + 1) * dk]
            kh = act[:, hd + h * dk:hd + (h + 1) * dk]
            vh = act[:, 2 * hd + h * dk:2 * hd + (h + 1) * dk]
            qh = qh * lax.rsqrt(jnp.sum(qh * qh, axis=-1, keepdims=True) + EPS) * (dk ** -0.5)
            kh = kh * lax.rsqrt(jnp.sum(kh * kh, axis=-1, keepdims=True) + EPS)
            gc = g[:, h:h + 1]
            gr = g_t[h:h + 1, :]
            bcol = beta[:, n_heads + h:n_heads + h + 1]
            rel = jnp.where(incl, jnp.exp(jnp.minimum(gc - gr, 0.0)), 0.0)
            probs.append(dict(b=b, h=h, q=qh, k=kh, v=vh, gc=gc, bcol=bcol, rel=rel, eg=jnp.exp(gc),
                              g_end=g[c - 1:c, h:h + 1]))
    for pr in probs:
        kb16 = pr["k"].astype(BF16)
        pr["a"] = jnp.where(strict, _dot_nt(kb16, kb16) * pr["rel"], 0.0) * pr["bcol"]
        pr["qk"] = _dot_nt(pr["q"], kb16) * pr["rel"]

    eye = jnp.where(r == s, 1.0, 0.0)
    same2 = (r // 2) == (s // 2)
    invs = [eye - jnp.where(same2, pr["a"], 0.0) for pr in probs]
    w = 2
    while w < c:
        lvl_mask = ((r // (2 * w)) == (s // (2 * w))) & ((r // w) != (s // w))
        xs = [_dot_hi(inv, jnp.where(lvl_mask, pr["a"], 0.0)) for inv, pr in zip(invs, probs)]
        invs = [inv - _dot_hi(x, inv) for inv, x in zip(invs, xs)]
        w *= 2
    ws = [_dot_hi(inv, pr["k"] * (pr["bcol"] * pr["eg"])) for inv, pr in zip(invs, probs)]
    us = [_dot_hi(inv, pr["v"] * pr["bcol"]) for inv, pr in zip(invs, probs)]

    for pr, wm, um in zip(probs, ws, us):
        b, h = pr["b"], pr["h"]
        hs = slice(h * dk, (h + 1) * dk)
        st = s_scr[b, h]
        st_bf = st.astype(BF16)
        v_new = um - _dot(wm, st_bf)
        o = _dot(pr["qk"], v_new) + _dot(pr["q"] * pr["eg"], st_bf)
        s_scr[b, h] = st * jnp.exp(pr["g_end"]) + _dot_tn(pr["k"] * jnp.exp(pr["g_end"] - pr["gc"]), v_new)
        o = _rms(o, onorm) * _silu(z_ref[b, :, hs])
        o_ref[b, :, hs] = o.astype(o_ref.dtype)

    @pl.when(c_idx == pl.num_programs(1) - 1)
    def _():
        s_ref[...] = s_scr[...]
        tail_ref[...] = xbuf[:, 0:pad, :]


def _gdn_prompt(qkv, z, ab, conv_w, a_log, dt_bias, onorm, bb):
    bsz, t, n3 = qkv.shape
    hd = z.shape[2]
    dk = hd // H_B
    c = min(CHUNK, t)
    assert t % c == 0 and bsz % bb == 0 and c >= 8
    blk = lambda n: pl.BlockSpec((bb, c, n), lambda i, j: (i, j, 0))
    o, s, tail = pl.pallas_call(
        functools.partial(_gdn_kernel, bb=bb, n_heads=H_B),
        grid=(bsz // bb, t // c),
        in_specs=[blk(n3), blk(hd), blk(LANES), _const_spec(conv_w.shape), _const_spec(a_log.shape),
                  _const_spec(dt_bias.shape), _const_spec(onorm.shape)],
        out_specs=[blk(hd), pl.BlockSpec((bb, H_B, dk, dk), lambda i, j: (i, 0, 0, 0)),
                   pl.BlockSpec((bb, 8, n3), lambda i, j: (i, 0, 0))],
        out_shape=[jax.ShapeDtypeStruct((bsz, t, hd), BF16), jax.ShapeDtypeStruct((bsz, H_B, dk, dk), F32),
                   jax.ShapeDtypeStruct((bsz, 8, n3), F32)],
        scratch_shapes=[pltpu.VMEM((bb, 8 + c, n3), F32), pltpu.VMEM((bb, H_B, dk, dk), F32)],
        compiler_params=_cparams("parallel", "arbitrary"), name="gdn_prompt")(qkv, z, ab, conv_w, a_log, dt_bias, onorm)
    return o, s, tail[:, 8 - (CONV_W - 1):, :]


def _sb_tiles(qm, kblk, vblks, bias2, u01, r_cols, mask):
    z = [jnp.dot(q, kblk, preferred_element_type=F32) + b for q, b in zip(qm, bias2)]
    sp = []
    for zz in z:
        s2 = jnp.maximum(zz, 0.0) + jnp.log2(1.0 + jnp.exp2(-jnp.abs(zz)))
        sp.append(s2 if mask is None else jnp.where(mask, s2, 0.0))
    inner = [jnp.dot(s2.astype(BF16), u01, preferred_element_type=F32) for s2 in sp]
    pv = None
    r_new = []
    for hh, (zz, s2, inn, r) in enumerate(zip(z, sp, inner, r_cols)):
        tot = s2 + inn
        w = jnp.exp2(zz - (tot + r))
        if mask is not None:
            w = jnp.where(mask, w, 0.0)
        r_new.append(r + tot[:, 0:1])
        d = _dot_nt(w, vblks[hh])
        pv = d if pv is None else pv + d
    return pv, r_new


def _sb_kernel(bias_ref, q_ref, k_ref, v_ref, o_ref, kb, vm, acc, rsum, *, hd, tk):
    p = pl.program_id(1)
    qi = pl.program_id(2)
    tq = q_ref.shape[1]
    t = k_ref.shape[2]
    hpp = LANES // hd
    row_head = lax.broadcasted_iota(jnp.int32, (LANES, tk), 0) // hd

    @pl.when(qi == 0)
    def _():
        for jj in range(t // tk):
            kb[jj] = k_ref[0, :, jj * tk:(jj + 1) * tk].astype(BF16)
            v = v_ref[0, :, jj * tk:(jj + 1) * tk]
            for hh in range(hpp):
                vm[hh, jj] = jnp.where(row_head == hh, v, 0.0).astype(BF16)

    lane_head_q = lax.broadcasted_iota(jnp.int32, (tq, LANES), 1) // hd
    qs = q_ref[0] * (hd ** -0.5 * LOG2E)
    qm = [jnp.where(lane_head_q == hh, qs, 0.0).astype(BF16) for hh in range(hpp)]
    bias2 = [bias_ref[p * hpp + hh] * LOG2E for hh in range(hpp)]
    rr = lax.broadcasted_iota(jnp.int32, (tk, tk), 0)
    ss = lax.broadcasted_iota(jnp.int32, (tk, tk), 1)
    u01 = jnp.where(rr > ss, 1.0, 0.0).astype(BF16)
    n_sub = tq // tk

    acc[...] = jnp.zeros_like(acc)
    rsum[...] = jnp.zeros_like(rsum)

    def tile(jj, mask):
        vblks = [vm[hh, jj] for hh in range(hpp)]
        pv, r_new = _sb_tiles(qm, kb[jj], vblks, bias2, u01, [rsum[hh] for hh in range(hpp)], mask)
        acc[...] += pv
        for hh in range(hpp):
            rsum[hh] = r_new[hh]

    qpos = lax.broadcasted_iota(jnp.int32, (tq, tk), 0)
    for sub in range(n_sub - 1, -1, -1):
        kpos = sub * tk + lax.broadcasted_iota(jnp.int32, (tq, tk), 1)
        tile(qi * n_sub + sub, kpos < qpos)

    def body(i, carry):
        for sub in range(n_sub):
            tile((qi - i) * n_sub - 1 - sub, None)
        return carry

    lax.fori_loop(0, qi, body, 0)
    o_ref[0] = acc[...].astype(o_ref.dtype)


def _sb_prompt(q, k_t, v_t, bias, tq=512, tk=256):
    bsz, t, hhd = q.shape
    hd = hhd // H_C
    tq, tk = min(tq, t), min(tk, t)
    assert t % tq == 0 and tq % tk == 0 and LANES % hd == 0
    n_pairs = hhd // LANES
    hpp = LANES // hd
    return pl.pallas_call(
        functools.partial(_sb_kernel, hd=hd, tk=tk),
        grid=(bsz, n_pairs, t // tq),
        in_specs=[pl.BlockSpec(memory_space=pltpu.SMEM),
                  pl.BlockSpec((1, tq, LANES), lambda b, p, i: (b, i, p)),
                  pl.BlockSpec((1, LANES, t), lambda b, p, i: (b, p, 0)),
                  pl.BlockSpec((1, LANES, t), lambda b, p, i: (b, p, 0))],
        out_specs=pl.BlockSpec((1, tq, LANES), lambda b, p, i: (b, i, p)),
        out_shape=jax.ShapeDtypeStruct((bsz, t, hhd), BF16),
        scratch_shapes=[pltpu.VMEM((t // tk, LANES, tk), BF16), pltpu.VMEM((hpp, t // tk, LANES, tk), BF16),
                        pltpu.VMEM((tq, LANES), F32), pltpu.VMEM((hpp, tq, 1), F32)],
        compiler_params=_cparams("parallel", "parallel", "arbitrary"), name="sb_prompt")(bias, q, k_t, v_t)


def _rg_kernel(x_ref, y_ref, cw_ref, cb_ref, wa_ref, ba_ref, wx_ref, bx_ref, lam_ref,
               o_ref, h_ref, tail_ref, xbuf, a_scr, b_scr, h_scr):
    ti = pl.program_id(1)
    tt = x_ref.shape[1]
    d = x_ref.shape[2]
    pad = 8

    @pl.when(ti == 0)
    def _():
        h_scr[...] = jnp.zeros_like(h_scr)
        xbuf[0:pad, :] = jnp.zeros((pad, d), F32)

    x = x_ref[0]
    cw = cw_ref[...]
    xbuf[pad:pad + tt, :] = x
    conv = x * cw[CONV_W - 1:CONV_W, :]
    for j in range(1, CONV_W):
        conv = conv + xbuf[pad - j:pad - j + tt, :] * cw[CONV_W - 1 - j:CONV_W - j, :]
    xbuf[0:pad, :] = x[tt - pad:tt, :]
    xc = conv + cb_ref[...]
    nsp = _softplus(-lam_ref[...])
    xb = xc.astype(BF16)
    rg = _sigmoid(jnp.dot(xb, wa_ref[...], preferred_element_type=F32) + ba_ref[...])
    ig = _sigmoid(jnp.dot(xb, wx_ref[...], preferred_element_type=F32) + bx_ref[...])
    log_a = (-RG_C) * rg * nsp
    a = jnp.exp(log_a)
    bt = jnp.sqrt(_one_minus_exp2(log_a, a)) * (ig * xc)
    pos = ti * tt + lax.broadcasted_iota(jnp.int32, (tt, d), 0)
    a = jnp.where(pos == 0, 0.0, a)
    bt = jnp.where(pos == 0, ig * xc, bt)

    row8 = lax.broadcasted_iota(jnp.int32, (tt, d), 0) % 8
    for sh in (1, 2, 4):
        a_prev = jnp.where(row8 >= sh, pltpu.roll(a, sh, 0), 1.0)
        b_prev = jnp.where(row8 >= sh, pltpu.roll(bt, sh, 0), 0.0)
        bt = a * b_prev + bt
        a = a * a_prev
    a_scr[...] = a
    b_scr[...] = bt

    def body(i, h):
        r0 = pl.multiple_of(i * 8, 8)
        hg = a_scr[pl.ds(r0, 8), :] * h + b_scr[pl.ds(r0, 8), :]
        b_scr[pl.ds(r0, 8), :] = hg
        return hg[7:8, :]

    h_last = lax.fori_loop(0, tt // 8, body, h_scr[...])
    h_scr[...] = h_last
    hs = b_scr[...]
    o_ref[0] = (hs * _gelu_tanh(y_ref[0])).astype(o_ref.dtype)

    @pl.when(ti == pl.num_programs(1) - 1)
    def _():
        h_ref[0] = h_last
        tail_ref[0] = xbuf[0:pad, :]


def _rg_prompt(xr, yr, conv_w, conv_b, wa_bd, b_a, wx_bd, b_x, lam, tt=512):
    bsz, t, d = xr.shape
    tt = min(tt, t)
    assert t % tt == 0 and tt % 8 == 0
    blk = pl.BlockSpec((1, tt, d), lambda b, i: (b, i, 0))
    consts = (conv_w, conv_b, wa_bd, b_a, wx_bd, b_x, lam)
    o, h, tail = pl.pallas_call(
        _rg_kernel,
        grid=(bsz, t // tt),
        in_specs=[blk, blk] + [_const_spec(a.shape) for a in consts],
        out_specs=[blk, pl.BlockSpec((1, 1, d), lambda b, i: (b, 0, 0)), pl.BlockSpec((1, 8, d), lambda b, i: (b, 0, 0))],
        out_shape=[jax.ShapeDtypeStruct((bsz, t, d), BF16), jax.ShapeDtypeStruct((bsz, 1, d), F32),
                   jax.ShapeDtypeStruct((bsz, 8, d), F32)],
        scratch_shapes=[pltpu.VMEM((8 + tt, d), F32), pltpu.VMEM((tt, d), F32), pltpu.VMEM((tt, d), F32),
                        pltpu.VMEM((1, d), F32)],
        compiler_params=_cparams("parallel", "arbitrary"), name="rglru_prompt")(xr, yr, *consts)
    return o, h[:, 0, :], tail[:, 8 - (CONV_W - 1):, :]


def _even_prep_kernel(x_ref, hist_ref, cw_ref, ab_ref, alog_ref, dtb_ref, lgp_ref,
                      q_ref, k_ref, v_ref, nh_ref, gb_ref, lg_ref, *, n_heads):
    x = x_ref[...]
    cw = cw_ref[...]
    conv = x * cw[CONV_W - 1:CONV_W, :]
    for j in range(CONV_W - 1):
        conv = conv + hist_ref[j] * cw[j:j + 1, :]
    for j in range(CONV_W - 2):
        nh_ref[j] = hist_ref[j + 1]
    nh_ref[CONV_W - 2] = x
    act = _silu(conv)
    hd = q_ref.shape[1]
    dk = hd // n_heads
    for h in range(n_heads):
        hs = slice(h * dk, (h + 1) * dk)
        qh = act[:, hs]
        kh = act[:, hd + h * dk:hd + (h + 1) * dk]
        q_ref[:, hs] = qh * lax.rsqrt(jnp.sum(qh * qh, axis=-1, keepdims=True) + EPS) * (dk ** -0.5)
        k_ref[:, hs] = kh * lax.rsqrt(jnp.sum(kh * kh, axis=-1, keepdims=True) + EPS)
    v_ref[...] = act[:, 2 * hd:3 * hd]
    ab = ab_ref[...]
    lane = lax.broadcasted_iota(jnp.int32, ab.shape, 1)
    lgb = -jnp.exp(alog_ref[...]) * _softplus(ab + dtb_ref[...])
    gb_ref[...] = jnp.where(lane < n_heads, lgb, _sigmoid(ab))
    lg_ref[...] = -_softplus(-lgp_ref[...]) * (1.0 / GLA_TAU)


def _even_state_kernel(qa_ref, ka_ref, lga_ref, va_ref, qb_ref, kb_ref, vb_ref, gb_ref, sa_ref, sb_ref,
                       oa_ref, ob_ref, sa_out, sb_out, *, nb, dk_a, dv_a, dk_b):
    for j in range(nb):
        gbrow = gb_ref[j:j + 1, :]
        for h in range(H_A):
            rs = slice(h * dk_a, (h + 1) * dk_a)
            s_new = sa_ref[j, h] * jnp.exp(lga_ref[0, rs, j:j + 1]) + ka_ref[0, rs, j:j + 1] * va_ref[j:j + 1, h * dv_a:(h + 1) * dv_a]
            sa_out[j, h] = s_new
            oa_ref[j:j + 1, h * dv_a:(h + 1) * dv_a] = jnp.sum(qa_ref[0, rs, j:j + 1] * (dk_a ** -0.5) * s_new, axis=0, keepdims=True)
        for h in range(H_B):
            rs = slice(h * dk_b, (h + 1) * dk_b)
            st = sb_ref[j, h]
            eg = jnp.exp(gbrow[:, h:h + 1])
            beta = gbrow[:, H_B + h:H_B + h + 1]
            kcol = kb_ref[0, rs, j:j + 1]
            vrow = vb_ref[j:j + 1, rs]
            v_new = vrow * beta - jnp.sum((kcol * (beta * eg)) * st, axis=0, keepdims=True)
            s_new = st * eg + kcol * v_new
            sb_out[j, h] = s_new
            ob_ref[j:j + 1, rs] = jnp.sum(qb_ref[0, rs, j:j + 1] * s_new, axis=0, keepdims=True)


def _even_out_kernel(oa_ref, ob_ref, r_ref, z_ref, ona_ref, onb_ref, o_ref, *, dv_a, dv_b):
    n_a = oa_ref.shape[1]
    for h in range(n_a // dv_a):
        sl = slice(h * dv_a, (h + 1) * dv_a)
        o_ref[:, sl] = (_rms(oa_ref[:, sl], ona_ref[...]) * _silu(r_ref[:, sl])).astype(o_ref.dtype)
    for h in range(ob_ref.shape[1] // dv_b):
        sl = slice(h * dv_b, (h + 1) * dv_b)
        o_ref[:, n_a + h * dv_b:n_a + (h + 1) * dv_b] = (_rms(ob_ref[:, sl], onb_ref[...]) * _silu(z_ref[:, sl])).astype(o_ref.dtype)


def _cols(x, nb):
    n, d = x.shape
    return x.reshape(n // nb, nb, d).transpose(0, 2, 1)


def _even_sample(qa, ka, va, ra, lgp, qkv, zb, ab, s_gla, s_gdn, hist, conv_w, a_log, dt_bias, on_a, on_b, nb=8):
    n, hdk_a = qa.shape
    hd_b = zb.shape[1]
    n3 = qkv.shape[1]
    hist_t = hist.transpose(1, 0, 2)
    full = lambda a: pl.BlockSpec(a.shape, lambda *_: (0,) * a.ndim)
    sds = lambda *s: jax.ShapeDtypeStruct(s, F32)
    ins = (qkv, hist_t, conv_w, ab, a_log, dt_bias, lgp)
    outs = [sds(n, hd_b), sds(n, hd_b), sds(n, hd_b), sds(CONV_W - 1, n, n3), sds(n, LANES), sds(n, hdk_a)]
    qb, kb, vb, new_hist_t, gb, lga = pl.pallas_call(
        functools.partial(_even_prep_kernel, n_heads=H_B),
        in_specs=[full(a) for a in ins], out_specs=[full(o) for o in outs], out_shape=outs,
        compiler_params=pltpu.CompilerParams(vmem_limit_bytes=VMEM_LIMIT_BYTES), name="even_sample_prep")(*ins)

    dk_a, dv_a, dk_b = hdk_a // H_A, va.shape[1] // H_A, hd_b // H_B
    assert n % nb == 0
    colspec = lambda d: pl.BlockSpec((1, d, nb), lambda i: (i, 0, 0))
    rowspec = lambda d: pl.BlockSpec((nb, d), lambda i: (i, 0))
    sa_spec = pl.BlockSpec((nb, H_A, dk_a, dv_a), lambda i: (i, 0, 0, 0))
    sb_spec = pl.BlockSpec((nb, H_B, dk_b, dk_b), lambda i: (i, 0, 0, 0))
    oa, ob, s_gla_new, s_gdn_new = pl.pallas_call(
        functools.partial(_even_state_kernel, nb=nb, dk_a=dk_a, dv_a=dv_a, dk_b=dk_b),
        grid=(n // nb,),
        in_specs=[colspec(hdk_a), colspec(hdk_a), colspec(hdk_a), rowspec(va.shape[1]),
                  colspec(hd_b), colspec(hd_b), rowspec(hd_b), rowspec(LANES), sa_spec, sb_spec],
        out_specs=[rowspec(va.shape[1]), rowspec(hd_b), sa_spec, sb_spec],
        out_shape=[sds(n, va.shape[1]), sds(n, hd_b), sds(*s_gla.shape), sds(*s_gdn.shape)],
        compiler_params=_cparams("parallel"), name="even_sample_state")(
            _cols(qa, nb), _cols(ka, nb), _cols(lga, nb), va, _cols(qb, nb), _cols(kb, nb), vb, gb, s_gla, s_gdn)

    ins = (oa, ob, ra, zb, on_a, on_b)
    out = jax.ShapeDtypeStruct((n, va.shape[1] + hd_b), BF16)
    o = pl.pallas_call(
        functools.partial(_even_out_kernel, dv_a=dv_a, dv_b=dk_b),
        in_specs=[full(a) for a in ins], out_specs=full(out), out_shape=out,
        compiler_params=pltpu.CompilerParams(vmem_limit_bytes=VMEM_LIMIT_BYTES), name="even_sample_out")(*ins)
    return o, s_gla_new, s_gdn_new, new_hist_t.transpose(1, 0, 2)


def _sb_decode_kernel(pt_ref, bias_ref, q_ref, *rest, n_pages):
    k_refs = rest[:n_pages]
    v_refs = rest[n_pages:2 * n_pages]
    o_ref = rest[2 * n_pages]
    hhd, page = k_refs[0].shape[1], k_refs[0].shape[2]
    n_heads = bias_ref.shape[0]
    hd = hhd // n_heads
    own = (lax.broadcasted_iota(jnp.int32, (n_heads, hhd), 0)
           == lax.broadcasted_iota(jnp.int32, (n_heads, hhd), 1) // hd)
    qrows = jnp.where(own, q_ref[0] * (hd ** -0.5 * LOG2E), 0.0).astype(BF16)
    rr = lax.broadcasted_iota(jnp.int32, (page, page), 0)
    ss = lax.broadcasted_iota(jnp.int32, (page, page), 1)
    u01 = jnp.where(rr > ss, 1.0, 0.0).astype(BF16)
    order = list(range(n_pages - 1, -1, -1))
    bias2 = jnp.concatenate([bias_ref[...] * LOG2E] * n_pages, axis=0)
    z_all = jnp.concatenate([_dot(qrows, k_refs[pg][0]) for pg in order], axis=0) + bias2
    sp = jnp.maximum(z_all, 0.0) + jnp.log2(1.0 + jnp.exp2(-jnp.abs(z_all)))
    tot = sp + jnp.dot(sp.astype(BF16), u01, preferred_element_type=F32)
    r_col = jnp.zeros((n_heads, 1), F32)
    ws = []
    for j in range(n_pages):
        rows = slice(j * n_heads, (j + 1) * n_heads)
        ws.append(jnp.exp2(z_all[rows] - (tot[rows] + r_col)))
        r_col = r_col + tot[rows, 0:1]
    acc = jnp.zeros((n_heads, hhd), F32)
    for w, pg in zip(ws, order):
        acc = acc + _dot_nt(w, v_refs[pg][0])
    o_ref[0] = jnp.sum(jnp.where(own, acc, 0.0), axis=0, keepdims=True)


def _sb_decode(q, cache_k, cache_v, page_table, bias):
    n, hhd = q.shape
    n_pool, page, n_heads, hd = cache_k.shape
    n_pages = page_table.shape[1]
    k_t = jnp.transpose(cache_k, (0, 2, 3, 1)).reshape(n_pool, hhd, page)
    v_t = jnp.transpose(cache_v, (0, 2, 3, 1)).reshape(n_pool, hhd, page)
    pt = page_table.reshape(-1)

    def page_spec(pg):
        return pl.BlockSpec((1, hhd, page), lambda b, pt_ref: (pt_ref[b * n_pages + pg], 0, 0))

    grid_spec = pltpu.PrefetchScalarGridSpec(
        num_scalar_prefetch=1, grid=(n,),
        in_specs=[pl.BlockSpec((n_heads, 1), lambda b, pt_ref: (0, 0)),
                  pl.BlockSpec((1, 1, hhd), lambda b, pt_ref: (b, 0, 0))]
                 + [page_spec(pg) for pg in range(n_pages)] * 2,
        out_specs=pl.BlockSpec((1, 1, hhd), lambda b, pt_ref: (b, 0, 0)))
    o = pl.pallas_call(
        functools.partial(_sb_decode_kernel, n_pages=n_pages),
        grid_spec=grid_spec, out_shape=jax.ShapeDtypeStruct((n, 1, hhd), F32),
        compiler_params=_cparams("parallel"), name="sb_decode")(
            pt, bias.reshape(n_heads, 1), q.reshape(n, 1, hhd), *([k_t] * n_pages), *([v_t] * n_pages))
    return o.reshape(n, hhd)


def _rg_decode_kernel(x_ref, y_ref, oc_ref, hist_ref, h0_ref, cw_ref, cb_ref, wa_ref, ba_ref, wx_ref, bx_ref, lam_ref,
                      o_ref, h_ref, nh_ref, *, reset):
    x = x_ref[...]
    cw = cw_ref[...]
    conv = x * cw[CONV_W - 1:CONV_W, :]
    for j in range(CONV_W - 1):
        conv = conv + hist_ref[j] * cw[j:j + 1, :]
    for j in range(CONV_W - 2):
        nh_ref[j] = hist_ref[j + 1]
    nh_ref[CONV_W - 2] = x
    xc = conv + cb_ref[...]
    xb = xc.astype(BF16)
    rg = _sigmoid(jnp.dot(xb, wa_ref[...], preferred_element_type=F32) + ba_ref[...])
    ig = _sigmoid(jnp.dot(xb, wx_ref[...], preferred_element_type=F32) + bx_ref[...])
    log_a = (-RG_C) * rg * _softplus(-lam_ref[...])
    if reset:
        h = ig * xc
    else:
        a = jnp.exp(log_a)
        h = a * h0_ref[...] + jnp.sqrt(_one_minus_exp2(log_a, a)) * (ig * xc)
    h_ref[...] = h
    d_c = oc_ref.shape[1]
    o_ref[:, 0:d_c] = oc_ref[...].astype(o_ref.dtype)
    o_ref[:, d_c:] = (h * _gelu_tanh(y_ref[...])).astype(o_ref.dtype)


def _odd_sample(oc, xr, yr, hist, h0, conv_w, conv_b, wa_bd, b_a, wx_bd, b_x, lam, offset):
    n, d = xr.shape
    hist_t = hist.transpose(1, 0, 2)
    full = lambda a: pl.BlockSpec(a.shape, lambda *_: (0,) * a.ndim)
    ins = (xr, yr, oc, hist_t, h0, conv_w, conv_b, wa_bd, b_a, wx_bd, b_x, lam)
    outs = [jax.ShapeDtypeStruct((n, oc.shape[1] + d), BF16), jax.ShapeDtypeStruct((n, d), F32),
            jax.ShapeDtypeStruct((CONV_W - 1, n, d), F32)]
    o, h, nh = pl.pallas_call(
        functools.partial(_rg_decode_kernel, reset=(offset == 0)),
        in_specs=[full(a) for a in ins], out_specs=[full(a) for a in outs], out_shape=outs,
        compiler_params=pltpu.CompilerParams(vmem_limit_bytes=VMEM_LIMIT_BYTES), name="rglru_decode")(*ins)
    return o, h, nh.transpose(1, 0, 2)


def _pad_cols(w, n):
    return jnp.pad(w, ((0, 0), (0, n - w.shape[1])))


def _block_diag(w):
    h, a, b = w.shape
    eye = jnp.eye(h, dtype=w.dtype)
    return (eye[:, None, :, None] * w[:, :, None, :]).reshape(h * a, h * b)


def _row(v, n=None):
    v = v.reshape(1, -1).astype(F32)
    return v if n is None else _pad_cols(v, n)


def kernel(x_prompt, x_sample, state_gla, state_gdn, state_gdn_conv, cache_sb_k, cache_sb_v, state_rg_h, state_rg_conv, page_table, ln_ffn, ffn_w_gate, ffn_w_up, ffn_w_down, ln_mix_e, w_in_e, gla_w_gate, gla_b_gate, gla_onorm, gdn_conv_w, gdn_a_log, gdn_dt_bias, gdn_onorm, w_out_e, ln_mix_o, w_in_o, sb_bias, rg_conv_w, rg_conv_b, rg_w_a, rg_b_a, rg_w_x, rg_b_x, rg_lambda, w_out_o, ln_final):
    bp, t, d = x_prompt.shape
    bs, ts, _ = x_sample.shape
    assert ts == 1
    depth = ln_ffn.shape[0]
    dk_a, dv_a = state_gla.shape[3], state_gla.shape[4]
    dk_b = state_gdn.shape[3]
    hd_c = cache_sb_k.shape[4]
    d_rnn = state_rg_h.shape[2]
    rank = gla_w_gate.shape[1]
    n_qa, n_va, n_b, n_c = H_A * dk_a, H_A * dv_a, H_B * dk_b, H_C * hd_c
    offset = page_table.shape[1] * cache_sb_k.shape[2]

    xp = x_prompt.reshape(bp * t, d)
    xs = x_sample.reshape(bs, d)
    outs = {}
    mix_p = mix_s = None
    for layer in range(depth):
        ffn = [(_row(ln_ffn[layer, i]), ffn_w_gate[layer, i].astype(BF16), ffn_w_up[layer, i].astype(BF16),
                ffn_w_down[layer, i].astype(BF16)) for i in range(2)]
        if layer % 2 == 0:
            e = layer // 2
            w = w_in_e[e]
            c0 = 0
            cols = []
            for n in (n_qa, n_qa, n_va, n_va, rank, 3 * n_b, n_b, 2 * H_B):
                cols.append(w[:, c0:c0 + n])
                c0 += n
            w_q, w_k, w_v, w_r, w_la, w_qkv, w_z, w_ab = cols
            w_list = [a.astype(BF16) for a in (w_q, w_k, w_v, w_r, w_qkv, w_z, _pad_cols(w_ab, LANES), _pad_cols(w_la, LANES))]
            gate = (jnp.pad(gla_w_gate[e], ((0, LANES - rank), (0, 0))).astype(BF16), _row(gla_b_gate[e]))
            proj = (_row(ln_mix_e[e]), w_list)
            w_out = w_out_e[e].astype(BF16)
        else:
            o = layer // 2
            w = w_in_o[o]
            w_list = [w[:, :n_c].astype(BF16), w[:, 3 * n_c:3 * n_c + d_rnn].astype(BF16),
                      w[:, 3 * n_c + d_rnn:].astype(BF16)]
            wt_list = [w[:, n_c:2 * n_c].T.astype(BF16), w[:, 2 * n_c:3 * n_c].T.astype(BF16)]
            gate = None
            proj = (_row(ln_mix_o[o]), w_list)
            w_out = w_out_o[o].astype(BF16)

        if layer % 2 == 0:
            xp, *pp = _block_call(xp, None, ffn[0], proj=proj, gate=gate)
            xs, *ps = _block_call(xs, None, ffn[0], proj=proj, gate=gate)
        else:
            xp, *pp = _block_call(xp, None, ffn[0], proj=proj, tproj=(wt_list, bp))
            xs, *ps = _block_call(xs, None, ffn[0], proj=proj, tproj=(wt_list, 1))

        if layer % 2 == 0:
            e = layer // 2
            r3 = lambda a: a.reshape(bp, t, a.shape[1])
            qa, ka, va, ra, qkv, zb, ab, lgp = pp
            on_a, on_b = _row(gla_onorm[e]), _row(gdn_onorm[e])
            a_log, dtb = _row(gdn_a_log[e], LANES), _row(gdn_dt_bias[e], LANES)
            oa, s_gla_p = _gla_prompt(r3(qa), r3(ka), r3(va), r3(ra), r3(lgp), on_a, bb=bp)
            ob, s_gdn_p, tail_p = _gdn_prompt(r3(qkv), r3(zb), r3(ab), gdn_conv_w[e], a_log, dtb, on_b, bb=bp)
            mix_p = ([oa.reshape(bp * t, n_va), ob.reshape(bp * t, n_b)], [w_out[:n_va], w_out[n_va:]])
            qa, ka, va, ra, qkv, zb, ab, lgp = ps
            o_s, s_gla_s, s_gdn_s, hist_s = _even_sample(qa, ka, va, ra, lgp, qkv, zb, ab, state_gla[e], state_gdn[e],
                                                         state_gdn_conv[e], gdn_conv_w[e], a_log, dtb, on_a, on_b)
            mix_s = ([o_s], [w_out])
            outs.setdefault("gla_p", []).append(s_gla_p)
            outs.setdefault("gla_s", []).append(s_gla_s)
            outs.setdefault("gdn_p", []).append(s_gdn_p)
            outs.setdefault("gdn_s", []).append(s_gdn_s)
            outs.setdefault("gcv_p", []).append(tail_p)
            outs.setdefault("gcv_s", []).append(hist_s)
        else:
            o = layer // 2
            r3 = lambda a: a.reshape(bp, t, a.shape[1])
            to_cache = lambda a: a.reshape(a.shape[0], H_C, hd_c, a.shape[2]).transpose(0, 3, 1, 2)
            qc, xr, yr, kc_t, vc_t = pp
            wa_bd, wx_bd = _block_diag(rg_w_a[o]).astype(BF16), _block_diag(rg_w_x[o]).astype(BF16)
            rg_consts = (rg_conv_w[o], _row(rg_conv_b[o]), wa_bd, _row(rg_b_a[o]), wx_bd, _row(rg_b_x[o]), _row(rg_lambda[o]))
            oc = _sb_prompt(r3(qc), kc_t, vc_t, sb_bias[o])
            od, h_p, tail_p = _rg_prompt(r3(xr), r3(yr), *rg_consts)
            mix_p = ([oc.reshape(bp * t, n_c), od.reshape(bp * t, d_rnn)], [w_out[:n_c], w_out[n_c:]])
            outs.setdefault("sbk_p", []).append(to_cache(kc_t))
            outs.setdefault("sbv_p", []).append(to_cache(vc_t))
            outs.setdefault("rgh_p", []).append(h_p)
            outs.setdefault("rgc_p", []).append(tail_p)
            qc, xr, yr, kc_t, vc_t = ps
            oc_s = _sb_decode(qc, cache_sb_k[o], cache_sb_v[o], page_table, sb_bias[o])
            o_s, h_s, hist_s = _odd_sample(oc_s, xr, yr, state_rg_conv[o], state_rg_h[o], *rg_consts, offset)
            mix_s = ([o_s], [w_out])
            outs.setdefault("sbk_s", []).append(to_cache(kc_t).reshape(bs, 1, H_C, hd_c))
            outs.setdefault("sbv_s", []).append(to_cache(vc_t).reshape(bs, 1, H_C, hd_c))
            outs.setdefault("rgh_s", []).append(h_s)
            outs.setdefault("rgc_s", []).append(hist_s)

        final_g = _row(ln_final) if layer == depth - 1 else None
        (xp,) = _block_call(xp, mix_p, ffn[1], final_g=final_g, tm=512)
        (xs,) = _block_call(xs, mix_s, ffn[1], final_g=final_g)

    st = lambda name: jnp.stack(outs[name])
    return (xp.reshape(bp, t, d), xs.reshape(bs, 1, d), st("gla_p"), st("gla_s"), st("gdn_p"), st("gdn_s"),
            st("gcv_p"), st("gcv_s"), st("sbk_p"), st("sbk_s"), st("sbv_p"), st("sbv_s"),
            st("rgh_p"), st("rgh_s"), st("rgc_p"), st("rgc_s"))
```

```python
import functools
import math

import jax
import jax.numpy as jnp
from jax import lax
from jax.experimental import pallas as pl
from jax.experimental.pallas import tpu as pltpu

F32 = jnp.float32
BF16 = jnp.bfloat16

H_A = 4
GLA_TAU = 16.0
H_B = 4
CONV_W = 4
H_C = 8
H_D = 8
RG_C = 8.0
CHUNK = 64
EPS = 1e-6
LOG2E = 1.4426950408889634

VMEM_LIMIT_BYTES = 56 * 1024 * 1024
LANES = 128


def _cparams(*sem):
    return pltpu.CompilerParams(dimension_semantics=sem, vmem_limit_bytes=VMEM_LIMIT_BYTES)


def _dot(a, b):
    return jnp.dot(a.astype(BF16), b.astype(BF16), preferred_element_type=F32)


def _dot_nt(a, b):
    return lax.dot_general(a.astype(BF16), b.astype(BF16), (((1,), (1,)), ((), ())), preferred_element_type=F32)


def _dot_tn(a, b):
    return lax.dot_general(a.astype(BF16), b.astype(BF16), (((0,), (0,)), ((), ())), preferred_element_type=F32)


def _split2(x):
    hi = x.astype(BF16)
    lo = (x - hi.astype(F32)).astype(BF16)
    return hi, lo


def _split3(x):
    x1 = x.astype(BF16)
    r1 = x - x1.astype(F32)
    x2 = r1.astype(BF16)
    x3 = (r1 - x2.astype(F32)).astype(BF16)
    return x1, x2, x3


def _dot_hi(a, b):
    a1, a2 = a if isinstance(a, tuple) else _split2(a)
    b1, b2 = b if isinstance(b, tuple) else _split2(b)
    d = lambda u, v: jnp.dot(u, v, preferred_element_type=F32)
    return d(a1, b1) + (d(a1, b2) + d(a2, b1))


def _mask_dot(m01, x):
    x1, x2, x3 = _split3(x)
    d = lambda v: jnp.dot(m01, v, preferred_element_type=F32)
    return d(x1) + (d(x2) + d(x3))


def _rms(x, g):
    return x * lax.rsqrt(jnp.mean(x * x, axis=-1, keepdims=True) + EPS) * g


def _sigmoid(x):
    return 1.0 / (1.0 + jnp.exp(-x))


def _silu(x):
    return x * _sigmoid(x)


def _softplus(x):
    return jnp.maximum(x, 0.0) + jnp.log1p(jnp.exp(-jnp.abs(x)))


def _gelu_tanh(x):
    return 0.5 * x * (1.0 + jnp.tanh(math.sqrt(2.0 / math.pi) * (x + 0.044715 * (x * x * x))))


def _one_minus_exp2(log_a, a):
    return jnp.tanh(-log_a) * (a * a + 1.0)


def _const_spec(shape):
    nd = len(shape)
    return pl.BlockSpec(shape, lambda *_: (0,) * nd, pipeline_mode=pl.Buffered(1))


def _block_kernel(*refs, n_mix, n_proj, n_tproj, has_gate, has_final, ff_chunk):
    it = iter(refs)
    x_ref = next(it)
    o_refs = [next(it) for _ in range(n_mix)]
    wo_refs = [next(it) for _ in range(n_mix)]
    gf_ref, wg_ref, wu_ref, wd_ref = next(it), next(it), next(it), next(it)
    if has_final:
        gfin_ref = next(it)
    if n_proj:
        gm_ref = next(it)
        wp_refs = [next(it) for _ in range(n_proj)]
    if has_gate:
        wgate_ref, bgate_ref = next(it), next(it)
    wt_refs = [next(it) for _ in range(n_tproj)]
    xo_ref = next(it)
    po_refs = [next(it) for _ in range(n_proj + int(has_gate))]
    tp_refs = [next(it) for _ in range(n_tproj)]

    x = x_ref[...]
    for o_ref, wo_ref in zip(o_refs, wo_refs):
        x = x + _dot(o_ref[...], wo_ref[...])
    h = _rms(x, gf_ref[...]).astype(BF16)
    d_ff = wg_ref.shape[1]
    acc = jnp.zeros_like(x)
    for c0 in range(0, d_ff, ff_chunk):
        gt = jnp.dot(h, wg_ref[:, c0:c0 + ff_chunk], preferred_element_type=F32)
        up = jnp.dot(h, wu_ref[:, c0:c0 + ff_chunk], preferred_element_type=F32)
        act = (_silu(gt) * up).astype(BF16)
        acc = acc + jnp.dot(act, wd_ref[c0:c0 + ff_chunk, :], preferred_element_type=F32)
    x = x + 0.5 * acc
    if has_final:
        xo_ref[...] = _rms(x, gfin_ref[...])
    else:
        xo_ref[...] = x
    if n_proj:
        hm = _rms(x, gm_ref[...]).astype(BF16)
        for j, wp_ref in enumerate(wp_refs):
            p = jnp.dot(hm, wp_ref[...], preferred_element_type=F32)
            if has_gate and j == n_proj - 1:
                po_refs[j][...] = p[:, :LANES]
                po_refs[j + 1][...] = _dot(p[:, LANES:], wgate_ref[...]) + bgate_ref[...]
            else:
                po_refs[j][...] = p
        for wt_ref, tp_ref in zip(wt_refs, tp_refs):
            tp_ref[0] = lax.dot_general(wt_ref[...], hm, (((1,), (1,)), ((), ())), preferred_element_type=F32)


def _block_call(x, mix, ffn, final_g=None, proj=None, gate=None, tproj=None, tm=256):
    m, d = x.shape
    tm = min(tm, m)
    assert m % tm == 0
    g_ffn, wg, wu, wd = ffn
    d_ff = wg.shape[1]
    ff_chunk = d_ff
    row = lambda n: pl.BlockSpec((tm, n), lambda i: (i, 0))
    args, specs = [x], [row(d)]
    o_list, wo_list = mix if mix is not None else ((), ())
    for o in o_list:
        args.append(o)
        specs.append(row(o.shape[1]))
    for w in wo_list:
        args.append(w)
        specs.append(_const_spec(w.shape))
    for a in (g_ffn, wg, wu, wd):
        args.append(a)
        specs.append(_const_spec(a.shape))
    if final_g is not None:
        args.append(final_g)
        specs.append(_const_spec(final_g.shape))
    out_shapes, out_specs = [jax.ShapeDtypeStruct((m, d), F32)], [row(d)]
    n_proj = 0
    if proj is not None:
        g_mix, w_list = proj
        n_proj = len(w_list)
        args.append(g_mix)
        specs.append(_const_spec(g_mix.shape))
        for w in w_list:
            args.append(w)
            specs.append(_const_spec(w.shape))
        widths = [w.shape[1] for w in w_list]
        if gate is not None:
            for a in gate:
                args.append(a)
                specs.append(_const_spec(a.shape))
            assert widths[-1] == 2 * LANES
            widths[-1:] = [LANES, gate[0].shape[1]]
        for n in widths:
            out_shapes.append(jax.ShapeDtypeStruct((m, n), F32))
            out_specs.append(row(n))
    n_tproj = 0
    if tproj is not None:
        assert proj is not None
        wt_list, bsz = tproj
        n_tproj = len(wt_list)
        t_rows = m // bsz
        assert t_rows % tm == 0
        nt = t_rows // tm
        for w in wt_list:
            args.append(w)
            specs.append(_const_spec(w.shape))
            out_shapes.append(jax.ShapeDtypeStruct((bsz, w.shape[0], t_rows), F32))
            out_specs.append(pl.BlockSpec((1, w.shape[0], tm), lambda i: (i // nt, 0, i % nt)))
    kern = functools.partial(_block_kernel, n_mix=len(o_list), n_proj=n_proj, n_tproj=n_tproj,
                             has_gate=gate is not None, has_final=final_g is not None, ff_chunk=ff_chunk)
    return pl.pallas_call(
        kern, grid=(m // tm,), in_specs=specs, out_specs=out_specs, out_shape=out_shapes,
        compiler_params=_cparams("parallel"), name="block")(*args)


def _tri_incl(c):
    r = lax.broadcasted_iota(jnp.int32, (c, c), 0)
    s = lax.broadcasted_iota(jnp.int32, (c, c), 1)
    return r >= s


def _gla_kernel(q_ref, k_ref, v_ref, r_ref, lg_ref, on_ref, o_ref, st_ref, s_scr, *, bb, dk, dv):
    c_idx = pl.program_id(1)
    c = q_ref.shape[1]
    n_pairs = q_ref.shape[2] // LANES
    hpp = LANES // dk

    @pl.when(c_idx == 0)
    def _():
        s_scr[...] = jnp.zeros_like(s_scr)

    incl = _tri_incl(c)
    tri01 = jnp.where(incl, 1.0, 0.0).astype(BF16)
    lane_head = lax.broadcasted_iota(jnp.int32, (c, LANES), 1) // dk
    onorm = on_ref[...]
    gs = [_mask_dot(tri01, -_softplus(-lg_ref[b]) * (1.0 / GLA_TAU)) for b in range(bb)]
    seqs = []
    for b, g in enumerate(gs):
        g_mid = g[c // 2 - 1:c // 2, :]
        g_end = g[c - 1:c, :]
        qs = q_ref[b] * (dk ** -0.5)
        kk = k_ref[b]
        seqs.append(dict(g_end=g_end,
                         q_state=qs * jnp.exp(g),
                         q_in=qs * jnp.exp(g - g_mid),
                         k_in=(kk * jnp.exp(g_mid - g)).astype(BF16),
                         k_end=kk * jnp.exp(g_end - g)))
    probs = []
    for b, sq in enumerate(seqs):
        for p in range(n_pairs):
            sl = slice(p * LANES, (p + 1) * LANES)
            for hh in range(hpp):
                sel = lane_head == hh
                att = _dot_nt(jnp.where(sel, sq["q_in"][:, sl], 0.0), sq["k_in"][:, sl])
                probs.append(dict(b=b, p=p, hh=hh, sl=sl, sel=sel, att=jnp.where(incl, att, 0.0).astype(BF16)))
    for pr in probs:
        b, p, hh, sl = pr["b"], pr["p"], pr["hh"], pr["sl"]
        vs = slice((p * hpp + hh) * dv, (p * hpp + hh + 1) * dv)
        vh = v_ref[b, :, vs].astype(BF16)
        o = _dot(pr["att"], vh) + _dot_nt(jnp.where(pr["sel"], seqs[b]["q_state"][:, sl], 0.0), s_scr[b, p])
        o = _rms(o, onorm) * _silu(r_ref[b, :, vs])
        o_ref[b, :, vs] = o.astype(o_ref.dtype)
        pr["upd"] = _dot_tn(vh, jnp.where(pr["sel"], seqs[b]["k_end"][:, sl], 0.0))
    for b in range(bb):
        for p in range(n_pairs):
            upd = [pr["upd"] for pr in probs if pr["b"] == b and pr["p"] == p]
            sl = slice(p * LANES, (p + 1) * LANES)
            s_scr[b, p] = s_scr[b, p] * jnp.exp(seqs[b]["g_end"][:, sl]) + sum(upd[1:], upd[0])

    @pl.when(c_idx == pl.num_programs(1) - 1)
    def _():
        st_ref[...] = s_scr[...]


def _gla_prompt(q, k, v, r, lg, onorm, bb):
    bsz, t, hdk = q.shape
    hdv = v.shape[2]
    dk, dv = hdk // H_A, hdv // H_A
    c = min(CHUNK, t)
    assert t % c == 0 and bsz % bb == 0 and LANES % dk == 0
    n_pairs = hdk // LANES
    blk = lambda n: pl.BlockSpec((bb, c, n), lambda i, j: (i, j, 0))
    o, st = pl.pallas_call(
        functools.partial(_gla_kernel, bb=bb, dk=dk, dv=dv),
        grid=(bsz // bb, t // c),
        in_specs=[blk(hdk), blk(hdk), blk(hdv), blk(hdv), blk(hdk), _const_spec(onorm.shape)],
        out_specs=[blk(hdv), pl.BlockSpec((bb, n_pairs, dv, LANES), lambda i, j: (i, 0, 0, 0))],
        out_shape=[jax.ShapeDtypeStruct((bsz, t, hdv), BF16), jax.ShapeDtypeStruct((bsz, n_pairs, dv, LANES), F32)],
        scratch_shapes=[pltpu.VMEM((bb, n_pairs, dv, LANES), F32)],
        compiler_params=_cparams("parallel", "arbitrary"), name="gla_prompt")(q, k, v, r, lg, onorm)
    hpp = LANES // dk
    s = st.reshape(bsz, n_pairs, dv, hpp, dk).transpose(0, 1, 3, 4, 2).reshape(bsz, H_A, dk, dv)
    return o, s


def _gdn_kernel(x_ref, z_ref, ab_ref, cw_ref, alog_ref, dtb_ref, on_ref, o_ref, s_ref, tail_ref, xbuf, s_scr,
                *, bb, n_heads):
    c_idx = pl.program_id(1)
    c = x_ref.shape[1]
    hd = z_ref.shape[2]
    dk = hd // n_heads
    pad = 8

    @pl.when(c_idx == 0)
    def _():
        s_scr[...] = jnp.zeros_like(s_scr)
        xbuf[:, 0:pad, :] = jnp.zeros((bb, pad, xbuf.shape[2]), F32)

    cp = c * n_heads
    t_p = lax.broadcasted_iota(jnp.int32, (c, cp), 0)
    s_p = lax.broadcasted_iota(jnp.int32, (c, cp), 1) % c
    h_p = lax.broadcasted_iota(jnp.int32, (c, cp), 1) // c
    incl_p = t_p >= s_p
    strict_p = t_p > s_p
    eye_p = t_p == s_p
    one16 = lambda m: jnp.where(m, 1.0, 0.0).astype(BF16)
    blk01 = [one16(h_p == h) for h in range(n_heads)]
    tri01 = one16(_tri_incl(c))
    ones_c = jnp.ones((c, c), BF16)
    e_row = lax.broadcasted_iota(jnp.int32, (LANES, cp), 0)
    e_head = lax.broadcasted_iota(jnp.int32, (LANES, cp), 1) // c
    expand_g = one16(e_row == e_head)
    expand_b = one16(e_row == e_head + n_heads)

    def blockdiag(x16, m01=None):
        return jnp.concatenate([x16 * (blk if m01 is None else blk * m01) for blk in blk01], axis=0)

    def lane_blocks(blocks, width):
        z = jnp.zeros_like(blocks[0])
        return jnp.concatenate([jnp.concatenate([blk if i == h else z for i in range(len(blocks))], axis=1)
                                for h, blk in enumerate(blocks)], axis=0)

    def sel3(x3, m01):
        d = lambda v: jnp.dot(v, m01, preferred_element_type=F32)
        return d(x3[0]) + (d(x3[1]) + d(x3[2]))

    cw = cw_ref[...]
    onorm = on_ref[...]
    neg_a = -jnp.exp(alog_ref[...])
    dtb = dtb_ref[...]
    seqs = []
    for b in range(bb):
        x = x_ref[b]
        xbuf[b, pad:pad + c, :] = x
        conv = x * cw[CONV_W - 1:CONV_W, :]
        for j in range(1, CONV_W):
            conv = conv + xbuf[b, pad - j:pad - j + c, :] * cw[CONV_W - 1 - j:CONV_W - j, :]
        xbuf[b, 0:pad, :] = x[c - pad:c, :]
        act = _silu(conv)
        ab = ab_ref[b]
        lgb = neg_a * _softplus(ab + dtb)
        beta = _sigmoid(ab)
        g = _mask_dot(tri01, lgb)
        qs, ks, vs, gcs, bcols = [], [], [], [], []
        for h in range(n_heads):
            qh = act[:, h * dk:(h + 1) * dk]
            kh = act[:, hd + h * dk:hd + (h + 1) * dk]
            qs.append(qh * lax.rsqrt(jnp.sum(qh * qh, axis=-1, keepdims=True) + EPS) * (dk ** -0.5))
            ks.append(kh * lax.rsqrt(jnp.sum(kh * kh, axis=-1, keepdims=True) + EPS))
            vs.append(act[:, 2 * hd + h * dk:2 * hd + (h + 1) * dk])
            gcs.append(g[:, h:h + 1])
            bcols.append(beta[:, n_heads + h:n_heads + h + 1])
        gcol_p = sel3(_split3(g), expand_g)
        bcol_p = sel3(_split3(beta), expand_b)
        grow_p = _mask_dot(ones_c, jnp.where(eye_p, gcol_p, 0.0))
        rel_p = jnp.where(incl_p, jnp.exp(jnp.minimum(gcol_p - grow_p, 0.0)), 0.0)
        k16 = [kh.astype(BF16) for kh in ks]
        kbd_t = lane_blocks(k16, dk)
        kk_p = _dot_nt(jnp.concatenate(k16, axis=1), kbd_t)
        qk_p = _dot_nt(jnp.concatenate(qs, axis=1), kbd_t) * rel_p
        a_p = jnp.where(strict_p, kk_p * rel_p, 0.0) * bcol_p
        seqs.append(dict(q=qs, k=ks, v=vs, gc=gcs, bcol=bcols, eg=[jnp.exp(gc) for gc in gcs],
                         g_end=[gc[c - 1:c, :] for gc in gcs], a=a_p, qk=qk_p.astype(BF16)))

    invs = [jnp.where(eye_p, 1.0, 0.0) - jnp.where((t_p // 2) == (s_p // 2), sq["a"], 0.0) for sq in seqs]
    a_splits = [_split2(sq["a"]) for sq in seqs]
    w = 2
    while w < c:
        lvl01 = one16(((t_p // (2 * w)) == (s_p // (2 * w))) & ((t_p // w) != (s_p // w)))
        inv_splits = [_split2(inv) for inv in invs]
        xs = [_dot_hi(isp, (blockdiag(ahi, lvl01), blockdiag(alo, lvl01)))
              for isp, (ahi, alo) in zip(inv_splits, a_splits)]
        invs = [inv - _dot_hi(x, (blockdiag(isp[0]), blockdiag(isp[1])))
                for inv, x, isp in zip(invs, xs, inv_splits)]
        w *= 2
    for sq, inv in zip(seqs, invs):
        isp = _split2(inv)
        sq["w"] = _dot_hi(isp, lane_blocks([k * (bc * eg) for k, bc, eg in zip(sq["k"], sq["bcol"], sq["eg"])], dk))
        sq["u"] = _dot_hi(isp, lane_blocks([v * bc for v, bc in zip(sq["v"], sq["bcol"])], dk))

    for b, sq in enumerate(seqs):
        sts = [s_scr[b, h] for h in range(n_heads)]
        sts16 = [st.astype(BF16) for st in sts]
        v_news = [sq["u"][:, h * dk:(h + 1) * dk] - _dot(sq["w"][:, h * dk:(h + 1) * dk], sts16[h])
                  for h in range(n_heads)]
        v_stack = jnp.concatenate(v_news, axis=0).astype(BF16)
        for h in range(n_heads):
            hs = slice(h * dk, (h + 1) * dk)
            o = (jnp.dot(sq["qk"] * blk01[h], v_stack, preferred_element_type=F32)
                 + _dot(sq["q"][h] * sq["eg"][h], sts16[h]))
            s_scr[b, h] = (sts[h] * jnp.exp(sq["g_end"][h])
                           + _dot_tn(sq["k"][h] * jnp.exp(sq["g_end"][h] - sq["gc"][h]), v_news[h]))
            o = _rms(o, onorm) * _silu(z_ref[b, :, hs])
            o_ref[b, :, hs] = o.astype(o_ref.dtype)

    @pl.when(c_idx == pl.num_programs(1) - 1)
    def _():
        s_ref[...] = s_scr[...]
        tail_ref[...] = xbuf[:, 0:pad, :]


def _gdn_prompt(qkv, z, ab, conv_w, a_log, dt_bias, onorm, bb):
    bsz, t, n3 = qkv.shape
    hd = z.shape[2]
    dk = hd // H_B
    c = min(CHUNK, t)
    assert t % c == 0 and bsz % bb == 0 and c >= 8
    blk = lambda n: pl.BlockSpec((bb, c, n), lambda i, j: (i, j, 0))
    o, s, tail = pl.pallas_call(
        functools.partial(_gdn_kernel, bb=bb, n_heads=H_B),
        grid=(bsz // bb, t // c),
        in_specs=[blk(n3), blk(hd), blk(LANES), _const_spec(conv_w.shape), _const_spec(a_log.shape),
                  _const_spec(dt_bias.shape), _const_spec(onorm.shape)],
        out_specs=[blk(hd), pl.BlockSpec((bb, H_B, dk, dk), lambda i, j: (i, 0, 0, 0)),
                   pl.BlockSpec((bb, 8, n3), lambda i, j: (i, 0, 0))],
        out_shape=[jax.ShapeDtypeStruct((bsz, t, hd), BF16), jax.ShapeDtypeStruct((bsz, H_B, dk, dk), F32),
                   jax.ShapeDtypeStruct((bsz, 8, n3), F32)],
        scratch_shapes=[pltpu.VMEM((bb, 8 + c, n3), F32), pltpu.VMEM((bb, H_B, dk, dk), F32)],
        compiler_params=_cparams("parallel", "arbitrary"), name="gdn_prompt")(qkv, z, ab, conv_w, a_log, dt_bias, onorm)
    return o, s, tail[:, 8 - (CONV_W - 1):, :]


def _sb_tiles(qm, kblk, vblks, bias2, u01, r_cols, mask):
    z = [jnp.dot(q, kblk, preferred_element_type=F32) + b for q, b in zip(qm, bias2)]
    sp = []
    for zz in z:
        s2 = jnp.maximum(zz, 0.0) + jnp.log2(1.0 + jnp.exp2(-jnp.abs(zz)))
        sp.append(s2 if mask is None else jnp.where(mask, s2, 0.0))
    inner = [jnp.dot(s2.astype(BF16), u01, preferred_element_type=F32) for s2 in sp]
    pv = None
    r_new = []
    for hh, (zz, s2, inn, r) in enumerate(zip(z, sp, inner, r_cols)):
        tot = s2 + inn
        w = jnp.exp2(zz - (tot + r))
        if mask is not None:
            w = jnp.where(mask, w, 0.0)
        r_new.append(r + tot[:, 0:1])
        d = _dot_nt(w, vblks[hh])
        pv = d if pv is None else pv + d
    return pv, r_new


def _sb_kernel(bias_ref, q_ref, k_ref, v_ref, o_ref, kb, vm, acc, rsum, *, hd, tk):
    p = pl.program_id(1)
    qi = pl.program_id(2)
    tq = q_ref.shape[1]
    t = k_ref.shape[2]
    hpp = LANES // hd
    row_head = lax.broadcasted_iota(jnp.int32, (LANES, tk), 0) // hd

    @pl.when(qi == 0)
    def _():
        for jj in range(t // tk):
            kb[jj] = k_ref[0, :, jj * tk:(jj + 1) * tk].astype(BF16)
            v = v_ref[0, :, jj * tk:(jj + 1) * tk]
            for hh in range(hpp):
                vm[hh, jj] = jnp.where(row_head == hh, v, 0.0).astype(BF16)

    lane_head_q = lax.broadcasted_iota(jnp.int32, (tq, LANES), 1) // hd
    qs = q_ref[0] * (hd ** -0.5 * LOG2E)
    qm = [jnp.where(lane_head_q == hh, qs, 0.0).astype(BF16) for hh in range(hpp)]
    bias2 = [bias_ref[p * hpp + hh] * LOG2E for hh in range(hpp)]
    rr = lax.broadcasted_iota(jnp.int32, (tk, tk), 0)
    ss = lax.broadcasted_iota(jnp.int32, (tk, tk), 1)
    u01 = jnp.where(rr > ss, 1.0, 0.0).astype(BF16)
    n_sub = tq // tk

    acc[...] = jnp.zeros_like(acc)
    rsum[...] = jnp.zeros_like(rsum)

    def tile(jj, mask, rows=slice(None)):
        vblks = [vm[hh, jj] for hh in range(hpp)]
        pv, r_new = _sb_tiles([q[rows] for q in qm], kb[jj], vblks, bias2, u01,
                              [rsum[hh, rows] for hh in range(hpp)], mask)
        acc[rows] += pv
        for hh in range(hpp):
            rsum[hh, rows] = r_new[hh]

    strict_lower = lax.broadcasted_iota(jnp.int32, (tk, tk), 1) < lax.broadcasted_iota(jnp.int32, (tk, tk), 0)
    for rb in range(n_sub):
        rows = slice(rb * tk, (rb + 1) * tk)
        tile(qi * n_sub + rb, strict_lower, rows)
        for sub in range(rb - 1, -1, -1):
            tile(qi * n_sub + sub, None, rows)

    def body(i, carry):
        for sub in range(n_sub):
            tile((qi - i) * n_sub - 1 - sub, None)
        return carry

    lax.fori_loop(0, qi, body, 0)
    o_ref[0] = acc[...].astype(o_ref.dtype)


def _sb_prompt(q, k_t, v_t, bias, tq=512, tk=256):
    bsz, t, hhd = q.shape
    hd = hhd // H_C
    tq, tk = min(tq, t), min(tk, t)
    assert t % tq == 0 and tq % tk == 0 and LANES % hd == 0
    n_pairs = hhd // LANES
    hpp = LANES // hd
    return pl.pallas_call(
        functools.partial(_sb_kernel, hd=hd, tk=tk),
        grid=(bsz, n_pairs, t // tq),
        in_specs=[pl.BlockSpec(memory_space=pltpu.SMEM),
                  pl.BlockSpec((1, tq, LANES), lambda b, p, i: (b, i, p)),
                  pl.BlockSpec((1, LANES, t), lambda b, p, i: (b, p, 0)),
                  pl.BlockSpec((1, LANES, t), lambda b, p, i: (b, p, 0))],
        out_specs=pl.BlockSpec((1, tq, LANES), lambda b, p, i: (b, i, p)),
        out_shape=jax.ShapeDtypeStruct((bsz, t, hhd), BF16),
        scratch_shapes=[pltpu.VMEM((t // tk, LANES, tk), BF16), pltpu.VMEM((hpp, t // tk, LANES, tk), BF16),
                        pltpu.VMEM((tq, LANES), F32), pltpu.VMEM((hpp, tq, 1), F32)],
        compiler_params=_cparams("parallel", "parallel", "arbitrary"), name="sb_prompt")(bias, q, k_t, v_t)


def _rg_kernel(x_ref, y_ref, cw_ref, cb_ref, wa_ref, ba_ref, wx_ref, bx_ref, lam_ref,
               o_ref, h_ref, tail_ref, xbuf, a_scr, b_scr, h_scr):
    ti = pl.program_id(1)
    tt = x_ref.shape[1]
    d = x_ref.shape[2]
    pad = 8

    @pl.when(ti == 0)
    def _():
        h_scr[...] = jnp.zeros_like(h_scr)
        xbuf[0:pad, :] = jnp.zeros((pad, d), F32)

    x = x_ref[0]
    cw = cw_ref[...]
    xbuf[pad:pad + tt, :] = x
    conv = x * cw[CONV_W - 1:CONV_W, :]
    for j in range(1, CONV_W):
        conv = conv + xbuf[pad - j:pad - j + tt, :] * cw[CONV_W - 1 - j:CONV_W - j, :]
    xbuf[0:pad, :] = x[tt - pad:tt, :]
    xc = conv + cb_ref[...]
    nsp = _softplus(-lam_ref[...])
    xb = xc.astype(BF16)
    rg = _sigmoid(jnp.dot(xb, wa_ref[...], preferred_element_type=F32) + ba_ref[...])
    ig = _sigmoid(jnp.dot(xb, wx_ref[...], preferred_element_type=F32) + bx_ref[...])
    log_a = (-RG_C) * rg * nsp
    a = jnp.exp(log_a)
    bt = jnp.sqrt(_one_minus_exp2(log_a, a)) * (ig * xc)
    pos = ti * tt + lax.broadcasted_iota(jnp.int32, (tt, d), 0)
    a = jnp.where(pos == 0, 0.0, a)
    bt = jnp.where(pos == 0, ig * xc, bt)

    row8 = lax.broadcasted_iota(jnp.int32, (tt, d), 0) % 8
    for sh in (1, 2, 4):
        a_prev = jnp.where(row8 >= sh, pltpu.roll(a, sh, 0), 1.0)
        b_prev = jnp.where(row8 >= sh, pltpu.roll(bt, sh, 0), 0.0)
        bt = a * b_prev + bt
        a = a * a_prev
    a_scr[...] = a
    b_scr[...] = bt

    def body(i, h):
        r0 = pl.multiple_of(i * 8, 8)
        hg = a_scr[pl.ds(r0, 8), :] * h + b_scr[pl.ds(r0, 8), :]
        b_scr[pl.ds(r0, 8), :] = hg
        return hg[7:8, :]

    h_last = lax.fori_loop(0, tt // 8, body, h_scr[...])
    h_scr[...] = h_last
    hs = b_scr[...]
    o_ref[0] = (hs * _gelu_tanh(y_ref[0])).astype(o_ref.dtype)

    @pl.when(ti == pl.num_programs(1) - 1)
    def _():
        h_ref[0] = h_last
        tail_ref[0] = xbuf[0:pad, :]


def _rg_prompt(xr, yr, conv_w, conv_b, wa_bd, b_a, wx_bd, b_x, lam, tt=512):
    bsz, t, d = xr.shape
    tt = min(tt, t)
    assert t % tt == 0 and tt % 8 == 0
    blk = pl.BlockSpec((1, tt, d), lambda b, i: (b, i, 0))
    consts = (conv_w, conv_b, wa_bd, b_a, wx_bd, b_x, lam)
    o, h, tail = pl.pallas_call(
        _rg_kernel,
        grid=(bsz, t // tt),
        in_specs=[blk, blk] + [_const_spec(a.shape) for a in consts],
        out_specs=[blk, pl.BlockSpec((1, 1, d), lambda b, i: (b, 0, 0)), pl.BlockSpec((1, 8, d), lambda b, i: (b, 0, 0))],
        out_shape=[jax.ShapeDtypeStruct((bsz, t, d), BF16), jax.ShapeDtypeStruct((bsz, 1, d), F32),
                   jax.ShapeDtypeStruct((bsz, 8, d), F32)],
        scratch_shapes=[pltpu.VMEM((8 + tt, d), F32), pltpu.VMEM((tt, d), F32), pltpu.VMEM((tt, d), F32),
                        pltpu.VMEM((1, d), F32)],
        compiler_params=_cparams("parallel", "arbitrary"), name="rglru_prompt")(xr, yr, *consts)
    return o, h[:, 0, :], tail[:, 8 - (CONV_W - 1):, :]


def _even_prep_kernel(x_ref, hist_ref, cw_ref, ab_ref, alog_ref, dtb_ref, lgp_ref,
                      q_ref, k_ref, v_ref, nh_ref, gb_ref, lg_ref, *, n_heads):
    x = x_ref[...]
    cw = cw_ref[...]
    conv = x * cw[CONV_W - 1:CONV_W, :]
    for j in range(CONV_W - 1):
        conv = conv + hist_ref[j] * cw[j:j + 1, :]
    for j in range(CONV_W - 2):
        nh_ref[j] = hist_ref[j + 1]
    nh_ref[CONV_W - 2] = x
    act = _silu(conv)
    hd = q_ref.shape[1]
    dk = hd // n_heads
    for h in range(n_heads):
        hs = slice(h * dk, (h + 1) * dk)
        qh = act[:, hs]
        kh = act[:, hd + h * dk:hd + (h + 1) * dk]
        q_ref[:, hs] = qh * lax.rsqrt(jnp.sum(qh * qh, axis=-1, keepdims=True) + EPS) * (dk ** -0.5)
        k_ref[:, hs] = kh * lax.rsqrt(jnp.sum(kh * kh, axis=-1, keepdims=True) + EPS)
    v_ref[...] = act[:, 2 * hd:3 * hd]
    ab = ab_ref[...]
    lane = lax.broadcasted_iota(jnp.int32, ab.shape, 1)
    lgb = -jnp.exp(alog_ref[...]) * _softplus(ab + dtb_ref[...])
    gb_ref[...] = jnp.where(lane < n_heads, lgb, _sigmoid(ab))
    lg_ref[...] = -_softplus(-lgp_ref[...]) * (1.0 / GLA_TAU)


def _even_state_kernel(qa_ref, ka_ref, lga_ref, va_ref, qb_ref, kb_ref, vb_ref, gb_ref, sa_ref, sb_ref,
                       oa_ref, ob_ref, sa_out, sb_out, *, nb, dk_a, dv_a, dk_b):
    for j in range(nb):
        gbrow = gb_ref[j:j + 1, :]
        for h in range(H_A):
            rs = slice(h * dk_a, (h + 1) * dk_a)
            s_new = sa_ref[j, h] * jnp.exp(lga_ref[0, rs, j:j + 1]) + ka_ref[0, rs, j:j + 1] * va_ref[j:j + 1, h * dv_a:(h + 1) * dv_a]
            sa_out[j, h] = s_new
            oa_ref[j:j + 1, h * dv_a:(h + 1) * dv_a] = jnp.sum(qa_ref[0, rs, j:j + 1] * (dk_a ** -0.5) * s_new, axis=0, keepdims=True)
        for h in range(H_B):
            rs = slice(h * dk_b, (h + 1) * dk_b)
            st = sb_ref[j, h]
            eg = jnp.exp(gbrow[:, h:h + 1])
            beta = gbrow[:, H_B + h:H_B + h + 1]
            kcol = kb_ref[0, rs, j:j + 1]
            vrow = vb_ref[j:j + 1, rs]
            v_new = vrow * beta - jnp.sum((kcol * (beta * eg)) * st, axis=0, keepdims=True)
            s_new = st * eg + kcol * v_new
            sb_out[j, h] = s_new
            ob_ref[j:j + 1, rs] = jnp.sum(qb_ref[0, rs, j:j + 1] * s_new, axis=0, keepdims=True)


def _even_out_kernel(oa_ref, ob_ref, r_ref, z_ref, ona_ref, onb_ref, o_ref, *, dv_a, dv_b):
    n_a = oa_ref.shape[1]
    for h in range(n_a // dv_a):
        sl = slice(h * dv_a, (h + 1) * dv_a)
        o_ref[:, sl] = (_rms(oa_ref[:, sl], ona_ref[...]) * _silu(r_ref[:, sl])).astype(o_ref.dtype)
    for h in range(ob_ref.shape[1] // dv_b):
        sl = slice(h * dv_b, (h + 1) * dv_b)
        o_ref[:, n_a + h * dv_b:n_a + (h + 1) * dv_b] = (_rms(ob_ref[:, sl], onb_ref[...]) * _silu(z_ref[:, sl])).astype(o_ref.dtype)


def _cols(x, nb):
    n, d = x.shape
    return x.reshape(n // nb, nb, d).transpose(0, 2, 1)


def _even_sample(qa, ka, va, ra, lgp, qkv, zb, ab, s_gla, s_gdn, hist, conv_w, a_log, dt_bias, on_a, on_b, nb=8):
    n, hdk_a = qa.shape
    hd_b = zb.shape[1]
    n3 = qkv.shape[1]
    hist_t = hist.transpose(1, 0, 2)
    full = lambda a: pl.BlockSpec(a.shape, lambda *_: (0,) * a.ndim)
    sds = lambda *s: jax.ShapeDtypeStruct(s, F32)
    ins = (qkv, hist_t, conv_w, ab, a_log, dt_bias, lgp)
    outs = [sds(n, hd_b), sds(n, hd_b), sds(n, hd_b), sds(CONV_W - 1, n, n3), sds(n, LANES), sds(n, hdk_a)]
    qb, kb, vb, new_hist_t, gb, lga = pl.pallas_call(
        functools.partial(_even_prep_kernel, n_heads=H_B),
        in_specs=[full(a) for a in ins], out_specs=[full(o) for o in outs], out_shape=outs,
        compiler_params=pltpu.CompilerParams(vmem_limit_bytes=VMEM_LIMIT_BYTES), name="even_sample_prep")(*ins)

    dk_a, dv_a, dk_b = hdk_a // H_A, va.shape[1] // H_A, hd_b // H_B
    assert n % nb == 0
    colspec = lambda d: pl.BlockSpec((1, d, nb), lambda i: (i, 0, 0))
    rowspec = lambda d: pl.BlockSpec((nb, d), lambda i: (i, 0))
    sa_spec = pl.BlockSpec((nb, H_A, dk_a, dv_a), lambda i: (i, 0, 0, 0))
    sb_spec = pl.BlockSpec((nb, H_B, dk_b, dk_b), lambda i: (i, 0, 0, 0))
    oa, ob, s_gla_new, s_gdn_new = pl.pallas_call(
        functools.partial(_even_state_kernel, nb=nb, dk_a=dk_a, dv_a=dv_a, dk_b=dk_b),
        grid=(n // nb,),
        in_specs=[colspec(hdk_a), colspec(hdk_a), colspec(hdk_a), rowspec(va.shape[1]),
                  colspec(hd_b), colspec(hd_b), rowspec(hd_b), rowspec(LANES), sa_spec, sb_spec],
        out_specs=[rowspec(va.shape[1]), rowspec(hd_b), sa_spec, sb_spec],
        out_shape=[sds(n, va.shape[1]), sds(n, hd_b), sds(*s_gla.shape), sds(*s_gdn.shape)],
        compiler_params=_cparams("parallel"), name="even_sample_state")(
            _cols(qa, nb), _cols(ka, nb), _cols(lga, nb), va, _cols(qb, nb), _cols(kb, nb), vb, gb, s_gla, s_gdn)

    ins = (oa, ob, ra, zb, on_a, on_b)
    out = jax.ShapeDtypeStruct((n, va.shape[1] + hd_b), BF16)
    o = pl.pallas_call(
        functools.partial(_even_out_kernel, dv_a=dv_a, dv_b=dk_b),
        in_specs=[full(a) for a in ins], out_specs=full(out), out_shape=out,
        compiler_params=pltpu.CompilerParams(vmem_limit_bytes=VMEM_LIMIT_BYTES), name="even_sample_out")(*ins)
    return o, s_gla_new, s_gdn_new, new_hist_t.transpose(1, 0, 2)


def _sb_decode_kernel(pt_ref, bias_ref, q_ref, *rest, n_pages):
    k_refs = rest[:n_pages]
    v_refs = rest[n_pages:2 * n_pages]
    o_ref = rest[2 * n_pages]
    hhd, page = k_refs[0].shape[1], k_refs[0].shape[2]
    n_heads = bias_ref.shape[0]
    hd = hhd // n_heads
    own = (lax.broadcasted_iota(jnp.int32, (n_heads, hhd), 0)
           == lax.broadcasted_iota(jnp.int32, (n_heads, hhd), 1) // hd)
    qrows = jnp.where(own, q_ref[0] * (hd ** -0.5 * LOG2E), 0.0).astype(BF16)
    rr = lax.broadcasted_iota(jnp.int32, (page, page), 0)
    ss = lax.broadcasted_iota(jnp.int32, (page, page), 1)
    u01 = jnp.where(rr > ss, 1.0, 0.0).astype(BF16)
    order = list(range(n_pages - 1, -1, -1))
    bias2 = jnp.concatenate([bias_ref[...] * LOG2E] * n_pages, axis=0)
    z_all = jnp.concatenate([_dot(qrows, k_refs[pg][0]) for pg in order], axis=0) + bias2
    sp = jnp.maximum(z_all, 0.0) + jnp.log2(1.0 + jnp.exp2(-jnp.abs(z_all)))
    tot = sp + jnp.dot(sp.astype(BF16), u01, preferred_element_type=F32)
    r_col = jnp.zeros((n_heads, 1), F32)
    ws = []
    for j in range(n_pages):
        rows = slice(j * n_heads, (j + 1) * n_heads)
        ws.append(jnp.exp2(z_all[rows] - (tot[rows] + r_col)))
        r_col = r_col + tot[rows, 0:1]
    acc = jnp.zeros((n_heads, hhd), F32)
    for w, pg in zip(ws, order):
        acc = acc + _dot_nt(w, v_refs[pg][0])
    o_ref[0] = jnp.sum(jnp.where(own, acc, 0.0), axis=0, keepdims=True)


def _sb_decode(q, cache_k, cache_v, page_table, bias):
    n, hhd = q.shape
    n_pool, page, n_heads, hd = cache_k.shape
    n_pages = page_table.shape[1]
    k_t = jnp.transpose(cache_k, (0, 2, 3, 1)).reshape(n_pool, hhd, page)
    v_t = jnp.transpose(cache_v, (0, 2, 3, 1)).reshape(n_pool, hhd, page)
    pt = page_table.reshape(-1)

    def page_spec(pg):
        return pl.BlockSpec((1, hhd, page), lambda b, pt_ref: (pt_ref[b * n_pages + pg], 0, 0))

    grid_spec = pltpu.PrefetchScalarGridSpec(
        num_scalar_prefetch=1, grid=(n,),
        in_specs=[pl.BlockSpec((n_heads, 1), lambda b, pt_ref: (0, 0)),
                  pl.BlockSpec((1, 1, hhd), lambda b, pt_ref: (b, 0, 0))]
                 + [page_spec(pg) for pg in range(n_pages)] * 2,
        out_specs=pl.BlockSpec((1, 1, hhd), lambda b, pt_ref: (b, 0, 0)))
    o = pl.pallas_call(
        functools.partial(_sb_decode_kernel, n_pages=n_pages),
        grid_spec=grid_spec, out_shape=jax.ShapeDtypeStruct((n, 1, hhd), F32),
        compiler_params=_cparams("parallel"), name="sb_decode")(
            pt, bias.reshape(n_heads, 1), q.reshape(n, 1, hhd), *([k_t] * n_pages), *([v_t] * n_pages))
    return o.reshape(n, hhd)


def _rg_decode_kernel(x_ref, y_ref, oc_ref, hist_ref, h0_ref, cw_ref, cb_ref, wa_ref, ba_ref, wx_ref, bx_ref, lam_ref,
                      o_ref, h_ref, nh_ref, *, reset):
    x = x_ref[...]
    cw = cw_ref[...]
    conv = x * cw[CONV_W - 1:CONV_W, :]
    for j in range(CONV_W - 1):
        conv = conv + hist_ref[j] * cw[j:j + 1, :]
    for j in range(CONV_W - 2):
        nh_ref[j] = hist_ref[j + 1]
    nh_ref[CONV_W - 2] = x
    xc = conv + cb_ref[...]
    xb = xc.astype(BF16)
    rg = _sigmoid(jnp.dot(xb, wa_ref[...], preferred_element_type=F32) + ba_ref[...])
    ig = _sigmoid(jnp.dot(xb, wx_ref[...], preferred_element_type=F32) + bx_ref[...])
    log_a = (-RG_C) * rg * _softplus(-lam_ref[...])
    if reset:
        h = ig * xc
    else:
        a = jnp.exp(log_a)
        h = a * h0_ref[...] + jnp.sqrt(_one_minus_exp2(log_a, a)) * (ig * xc)
    h_ref[...] = h
    d_c = oc_ref.shape[1]
    o_ref[:, 0:d_c] = oc_ref[...].astype(o_ref.dtype)
    o_ref[:, d_c:] = (h * _gelu_tanh(y_ref[...])).astype(o_ref.dtype)


def _odd_sample(oc, xr, yr, hist, h0, conv_w, conv_b, wa_bd, b_a, wx_bd, b_x, lam, offset):
    n, d = xr.shape
    hist_t = hist.transpose(1, 0, 2)
    full = lambda a: pl.BlockSpec(a.shape, lambda *_: (0,) * a.ndim)
    ins = (xr, yr, oc, hist_t, h0, conv_w, conv_b, wa_bd, b_a, wx_bd, b_x, lam)
    outs = [jax.ShapeDtypeStruct((n, oc.shape[1] + d), BF16), jax.ShapeDtypeStruct((n, d), F32),
            jax.ShapeDtypeStruct((CONV_W - 1, n, d), F32)]
    o, h, nh = pl.pallas_call(
        functools.partial(_rg_decode_kernel, reset=(offset == 0)),
        in_specs=[full(a) for a in ins], out_specs=[full(a) for a in outs], out_shape=outs,
        compiler_params=pltpu.CompilerParams(vmem_limit_bytes=VMEM_LIMIT_BYTES), name="rglru_decode")(*ins)
    return o, h, nh.transpose(1, 0, 2)


def _pad_cols(w, n):
    return jnp.pad(w, ((0, 0), (0, n - w.shape[1])))


def _block_diag(w):
    h, a, b = w.shape
    eye = jnp.eye(h, dtype=w.dtype)
    return (eye[:, None, :, None] * w[:, :, None, :]).reshape(h * a, h * b)


def _row(v, n=None):
    v = v.reshape(1, -1).astype(F32)
    return v if n is None else _pad_cols(v, n)


def kernel(x_prompt, x_sample, state_gla, state_gdn, state_gdn_conv, cache_sb_k, cache_sb_v, state_rg_h, state_rg_conv, page_table, ln_ffn, ffn_w_gate, ffn_w_up, ffn_w_down, ln_mix_e, w_in_e, gla_w_gate, gla_b_gate, gla_onorm, gdn_conv_w, gdn_a_log, gdn_dt_bias, gdn_onorm, w_out_e, ln_mix_o, w_in_o, sb_bias, rg_conv_w, rg_conv_b, rg_w_a, rg_b_a, rg_w_x, rg_b_x, rg_lambda, w_out_o, ln_final):
    bp, t, d = x_prompt.shape
    bs, ts, _ = x_sample.shape
    assert ts == 1
    depth = ln_ffn.shape[0]
    dk_a, dv_a = state_gla.shape[3], state_gla.shape[4]
    dk_b = state_gdn.shape[3]
    hd_c = cache_sb_k.shape[4]
    d_rnn = state_rg_h.shape[2]
    rank = gla_w_gate.shape[1]
    n_qa, n_va, n_b, n_c = H_A * dk_a, H_A * dv_a, H_B * dk_b, H_C * hd_c
    offset = page_table.shape[1] * cache_sb_k.shape[2]

    xp = x_prompt.reshape(bp * t, d)
    xs = x_sample.reshape(bs, d)
    outs = {}
    mix_p = mix_s = None
    for layer in range(depth):
        ffn = [(_row(ln_ffn[layer, i]), ffn_w_gate[layer, i].astype(BF16), ffn_w_up[layer, i].astype(BF16),
                ffn_w_down[layer, i].astype(BF16)) for i in range(2)]
        if layer % 2 == 0:
            e = layer // 2
            w = w_in_e[e]
            c0 = 0
            cols = []
            for n in (n_qa, n_qa, n_va, n_va, rank, 3 * n_b, n_b, 2 * H_B):
                cols.append(w[:, c0:c0 + n])
                c0 += n
            w_q, w_k, w_v, w_r, w_la, w_qkv, w_z, w_ab = cols
            w_abla = jnp.concatenate([_pad_cols(w_ab, LANES), _pad_cols(w_la, LANES)], axis=1)
            w_list = [a.astype(BF16) for a in (w_q, w_k, w_v, w_r, w_qkv, w_z, w_abla)]
            gate = (jnp.pad(gla_w_gate[e], ((0, LANES - rank), (0, 0))).astype(BF16), _row(gla_b_gate[e]))
            proj = (_row(ln_mix_e[e]), w_list)
            w_out = w_out_e[e].astype(BF16)
        else:
            o = layer // 2
            w = w_in_o[o]
            w_list = [w[:, :n_c].astype(BF16), w[:, 3 * n_c:3 * n_c + d_rnn].astype(BF16),
                      w[:, 3 * n_c + d_rnn:].astype(BF16)]
            wt_list = [w[:, n_c:2 * n_c].T.astype(BF16), w[:, 2 * n_c:3 * n_c].T.astype(BF16)]
            gate = None
            proj = (_row(ln_mix_o[o]), w_list)
            w_out = w_out_o[o].astype(BF16)

        if layer % 2 == 0:
            xp, *pp = _block_call(xp, None, ffn[0], proj=proj, gate=gate)
            xs, *ps = _block_call(xs, None, ffn[0], proj=proj, gate=gate)
        else:
            xp, *pp = _block_call(xp, None, ffn[0], proj=proj, tproj=(wt_list, bp))
            xs, *ps = _block_call(xs, None, ffn[0], proj=proj, tproj=(wt_list, 1))

        if layer % 2 == 0:
            e = layer // 2
            r3 = lambda a: a.reshape(bp, t, a.shape[1])
            qa, ka, va, ra, qkv, zb, ab, lgp = pp
            on_a, on_b = _row(gla_onorm[e]), _row(gdn_onorm[e])
            a_log, dtb = _row(gdn_a_log[e], LANES), _row(gdn_dt_bias[e], LANES)
            oa, s_gla_p = _gla_prompt(r3(qa), r3(ka), r3(va), r3(ra), r3(lgp), on_a, bb=bp)
            ob, s_gdn_p, tail_p = _gdn_prompt(r3(qkv), r3(zb), r3(ab), gdn_conv_w[e], a_log, dtb, on_b, bb=bp)
            mix_p = ([oa.reshape(bp * t, n_va), ob.reshape(bp * t, n_b)], [w_out[:n_va], w_out[n_va:]])
            qa, ka, va, ra, qkv, zb, ab, lgp = ps
            o_s, s_gla_s, s_gdn_s, hist_s = _even_sample(qa, ka, va, ra, lgp, qkv, zb, ab, state_gla[e], state_gdn[e],
                                                         state_gdn_conv[e], gdn_conv_w[e], a_log, dtb, on_a, on_b)
            mix_s = ([o_s], [w_out])
            outs.setdefault("gla_p", []).append(s_gla_p)
            outs.setdefault("gla_s", []).append(s_gla_s)
            outs.setdefault("gdn_p", []).append(s_gdn_p)
            outs.setdefault("gdn_s", []).append(s_gdn_s)
            outs.setdefault("gcv_p", []).append(tail_p)
            outs.setdefault("gcv_s", []).append(hist_s)
        else:
            o = layer // 2
            r3 = lambda a: a.reshape(bp, t, a.shape[1])
            to_cache = lambda a: a.reshape(a.shape[0], H_C, hd_c, a.shape[2]).transpose(0, 3, 1, 2)
            qc, xr, yr, kc_t, vc_t = pp
            wa_bd, wx_bd = _block_diag(rg_w_a[o]).astype(BF16), _block_diag(rg_w_x[o]).astype(BF16)
            rg_consts = (rg_conv_w[o], _row(rg_conv_b[o]), wa_bd, _row(rg_b_a[o]), wx_bd, _row(rg_b_x[o]), _row(rg_lambda[o]))
            oc = _sb_prompt(r3(qc), kc_t, vc_t, sb_bias[o])
            od, h_p, tail_p = _rg_prompt(r3(xr), r3(yr), *rg_consts)
            mix_p = ([oc.reshape(bp * t, n_c), od.reshape(bp * t, d_rnn)], [w_out[:n_c], w_out[n_c:]])
            outs.setdefault("sbk_p", []).append(to_cache(kc_t))
            outs.setdefault("sbv_p", []).append(to_cache(vc_t))
            outs.setdefault("rgh_p", []).append(h_p)
            outs.setdefault("rgc_p", []).append(tail_p)
            qc, xr, yr, kc_t, vc_t = ps
            oc_s = _sb_decode(qc, cache_sb_k[o], cache_sb_v[o], page_table, sb_bias[o])
            o_s, h_s, hist_s = _odd_sample(oc_s, xr, yr, state_rg_conv[o], state_rg_h[o], *rg_consts, offset)
            mix_s = ([o_s], [w_out])
            outs.setdefault("sbk_s", []).append(to_cache(kc_t).reshape(bs, 1, H_C, hd_c))
            outs.setdefault("sbv_s", []).append(to_cache(vc_t).reshape(bs, 1, H_C, hd_c))
            outs.setdefault("rgh_s", []).append(h_s)
            outs.setdefault("rgc_s", []).append(hist_s)

        final_g = _row(ln_final) if layer == depth - 1 else None
        (xp,) = _block_call(xp, mix_p, ffn[1], final_g=final_g, tm=512)
        (xs,) = _block_call(xs, mix_s, ffn[1], final_g=final_g)

    st = lambda name: jnp.stack(outs[name])
    return (xp.reshape(bp, t, d), xs.reshape(bs, 1, d), st("gla_p"), st("gla_s"), st("gdn_p"), st("gdn_s"),
            st("gcv_p"), st("gcv_s"), st("sbk_p"), st("sbk_s"), st("sbv_p"), st("sbv_s"),
            st("rgh_p"), st("rgh_s"), st("rgc_p"), st("rgc_s"))
```

```python
import functools
import math

import jax
import jax.numpy as jnp
from jax import lax
from jax.experimental import pallas as pl
from jax.experimental.pallas import tpu as pltpu

F32 = jnp.float32
BF16 = jnp.bfloat16

H_A = 4
GLA_TAU = 16.0
H_B = 4
CONV_W = 4
H_C = 8
H_D = 8
RG_C = 8.0
CHUNK = 64
EPS = 1e-6
LOG2E = 1.4426950408889634

VMEM_LIMIT_BYTES = 56 * 1024 * 1024
LANES = 128
SUBLANES = 8

TM_PROJ = 256
TM_POST = 512
SB_TQ, SB_TK = 512, 256
RG_TT = 512
SAMPLE_SEQS = 8


def _cparams(*sem):
    return pltpu.CompilerParams(dimension_semantics=sem, vmem_limit_bytes=VMEM_LIMIT_BYTES)


def _dot(a, b):
    return jnp.dot(a.astype(BF16), b.astype(BF16), preferred_element_type=F32)


def _dot_nt(a, b):
    return lax.dot_general(a.astype(BF16), b.astype(BF16), (((1,), (1,)), ((), ())), preferred_element_type=F32)


def _dot_tn(a, b):
    return lax.dot_general(a.astype(BF16), b.astype(BF16), (((0,), (0,)), ((), ())), preferred_element_type=F32)


def _split2(x):
    hi = x.astype(BF16)
    lo = (x - hi.astype(F32)).astype(BF16)
    return hi, lo


def _split3(x):
    x1 = x.astype(BF16)
    r1 = x - x1.astype(F32)
    x2 = r1.astype(BF16)
    x3 = (r1 - x2.astype(F32)).astype(BF16)
    return x1, x2, x3


def _dot_hi(a, b):
    a1, a2 = a if isinstance(a, tuple) else _split2(a)
    b1, b2 = b if isinstance(b, tuple) else _split2(b)
    d = lambda u, v: jnp.dot(u, v, preferred_element_type=F32)
    return d(a1, b1) + (d(a1, b2) + d(a2, b1))


def _mask_dot(m01, x):
    x1, x2, x3 = _split3(x)
    d = lambda v: jnp.dot(m01, v, preferred_element_type=F32)
    return d(x1) + (d(x2) + d(x3))


def _rms(x, g):
    return x * lax.rsqrt(jnp.mean(x * x, axis=-1, keepdims=True) + EPS) * g


def _sigmoid(x):
    return 1.0 / (1.0 + jnp.exp(-x))


def _silu(x):
    return x * _sigmoid(x)


def _softplus(x):
    return jnp.maximum(x, 0.0) + jnp.log1p(jnp.exp(-jnp.abs(x)))


def _gelu_tanh(x):
    return 0.5 * x * (1.0 + jnp.tanh(math.sqrt(2.0 / math.pi) * (x + 0.044715 * (x * x * x))))


def _one_minus_exp2(log_a, a):
    return jnp.tanh(-log_a) * (a * a + 1.0)


def _const_spec(shape):
    nd = len(shape)
    return pl.BlockSpec(shape, lambda *_: (0,) * nd, pipeline_mode=pl.Buffered(1))


def _block_kernel(*refs, n_mix, n_proj, n_tproj, has_gate, has_final, ff_chunk):
    it = iter(refs)
    x_ref = next(it)
    o_refs = [next(it) for _ in range(n_mix)]
    wo_refs = [next(it) for _ in range(n_mix)]
    gf_ref, wg_ref, wu_ref, wd_ref = next(it), next(it), next(it), next(it)
    if has_final:
        gfin_ref = next(it)
    if n_proj:
        gm_ref = next(it)
        wp_refs = [next(it) for _ in range(n_proj)]
    if has_gate:
        wgate_ref, bgate_ref = next(it), next(it)
    wt_refs = [next(it) for _ in range(n_tproj)]
    xo_ref = next(it)
    po_refs = [next(it) for _ in range(n_proj + int(has_gate))]
    tp_refs = [next(it) for _ in range(n_tproj)]

    x = x_ref[...]
    for o_ref, wo_ref in zip(o_refs, wo_refs):
        x = x + _dot(o_ref[...], wo_ref[...])
    h = _rms(x, gf_ref[...]).astype(BF16)
    d_ff = wg_ref.shape[1]
    acc = jnp.zeros_like(x)
    for c0 in range(0, d_ff, ff_chunk):
        gt = jnp.dot(h, wg_ref[:, c0:c0 + ff_chunk], preferred_element_type=F32)
        up = jnp.dot(h, wu_ref[:, c0:c0 + ff_chunk], preferred_element_type=F32)
        act = (_silu(gt) * up).astype(BF16)
        acc = acc + jnp.dot(act, wd_ref[c0:c0 + ff_chunk, :], preferred_element_type=F32)
    x = x + 0.5 * acc
    if has_final:
        xo_ref[...] = _rms(x, gfin_ref[...])
    else:
        xo_ref[...] = x
    if n_proj:
        hm = _rms(x, gm_ref[...]).astype(BF16)
        for j, wp_ref in enumerate(wp_refs):
            p = jnp.dot(hm, wp_ref[...], preferred_element_type=F32)
            if has_gate and j == n_proj - 1:
                po_refs[j][...] = p[:, :LANES]
                po_refs[j + 1][...] = _dot(p[:, LANES:], wgate_ref[...]) + bgate_ref[...]
            else:
                po_refs[j][...] = p
        for wt_ref, tp_ref in zip(wt_refs, tp_refs):
            tp_ref[0] = lax.dot_general(wt_ref[...], hm, (((1,), (1,)), ((), ())), preferred_element_type=F32)


def _block_call(x, mix, ffn, final_g=None, proj=None, gate=None, tproj=None, tm=TM_PROJ):
    m, d = x.shape
    tm = min(tm, m)
    assert m % tm == 0
    g_ffn, wg, wu, wd = ffn
    d_ff = wg.shape[1]
    ff_chunk = d_ff
    row = lambda n: pl.BlockSpec((tm, n), lambda i: (i, 0))
    args, specs = [x], [row(d)]
    o_list, wo_list = mix if mix is not None else ((), ())
    for o in o_list:
        args.append(o)
        specs.append(row(o.shape[1]))
    for w in wo_list:
        args.append(w)
        specs.append(_const_spec(w.shape))
    for a in (g_ffn, wg, wu, wd):
        args.append(a)
        specs.append(_const_spec(a.shape))
    if final_g is not None:
        args.append(final_g)
        specs.append(_const_spec(final_g.shape))
    out_shapes, out_specs = [jax.ShapeDtypeStruct((m, d), F32)], [row(d)]
    n_proj = 0
    if proj is not None:
        g_mix, w_list = proj
        n_proj = len(w_list)
        args.append(g_mix)
        specs.append(_const_spec(g_mix.shape))
        for w in w_list:
            args.append(w)
            specs.append(_const_spec(w.shape))
        widths = [w.shape[1] for w in w_list]
        if gate is not None:
            for a in gate:
                args.append(a)
                specs.append(_const_spec(a.shape))
            assert widths[-1] == 2 * LANES
            widths[-1:] = [LANES, gate[0].shape[1]]
        for n in widths:
            out_shapes.append(jax.ShapeDtypeStruct((m, n), F32))
            out_specs.append(row(n))
    n_tproj = 0
    if tproj is not None:
        assert proj is not None
        wt_list, bsz = tproj
        n_tproj = len(wt_list)
        t_rows = m // bsz
        assert t_rows % tm == 0
        nt = t_rows // tm
        for w in wt_list:
            args.append(w)
            specs.append(_const_spec(w.shape))
            out_shapes.append(jax.ShapeDtypeStruct((bsz, w.shape[0], t_rows), F32))
            out_specs.append(pl.BlockSpec((1, w.shape[0], tm), lambda i: (i // nt, 0, i % nt)))
    kern = functools.partial(_block_kernel, n_mix=len(o_list), n_proj=n_proj, n_tproj=n_tproj,
                             has_gate=gate is not None, has_final=final_g is not None, ff_chunk=ff_chunk)
    return pl.pallas_call(
        kern, grid=(m // tm,), in_specs=specs, out_specs=out_specs, out_shape=out_shapes,
        compiler_params=_cparams("parallel"), name="block")(*args)


def _tri_incl(c):
    r = lax.broadcasted_iota(jnp.int32, (c, c), 0)
    s = lax.broadcasted_iota(jnp.int32, (c, c), 1)
    return r >= s


def _gla_kernel(q_ref, k_ref, v_ref, r_ref, lg_ref, on_ref, o_ref, st_ref, s_scr, *, bb, dk, dv):
    c_idx = pl.program_id(1)
    c = q_ref.shape[1]
    n_pairs = q_ref.shape[2] // LANES
    hpp = LANES // dk

    @pl.when(c_idx == 0)
    def _():
        s_scr[...] = jnp.zeros_like(s_scr)

    incl = _tri_incl(c)
    tri01 = jnp.where(incl, 1.0, 0.0).astype(BF16)
    lane_head = lax.broadcasted_iota(jnp.int32, (c, LANES), 1) // dk
    onorm = on_ref[...]
    gs = [_mask_dot(tri01, -_softplus(-lg_ref[b]) * (1.0 / GLA_TAU)) for b in range(bb)]
    seqs = []
    for b, g in enumerate(gs):
        g_mid = g[c // 2 - 1:c // 2, :]
        g_end = g[c - 1:c, :]
        qs = q_ref[b] * (dk ** -0.5)
        kk = k_ref[b]
        seqs.append(dict(g_end=g_end,
                         q_state=qs * jnp.exp(g),
                         q_in=qs * jnp.exp(g - g_mid),
                         k_in=(kk * jnp.exp(g_mid - g)).astype(BF16),
                         k_end=kk * jnp.exp(g_end - g)))
    probs = []
    for b, sq in enumerate(seqs):
        for p in range(n_pairs):
            sl = slice(p * LANES, (p + 1) * LANES)
            for hh in range(hpp):
                sel = lane_head == hh
                att = _dot_nt(jnp.where(sel, sq["q_in"][:, sl], 0.0), sq["k_in"][:, sl])
                probs.append(dict(b=b, p=p, hh=hh, sl=sl, sel=sel, att=jnp.where(incl, att, 0.0).astype(BF16)))
    for pr in probs:
        b, p, hh, sl = pr["b"], pr["p"], pr["hh"], pr["sl"]
        vs = slice((p * hpp + hh) * dv, (p * hpp + hh + 1) * dv)
        vh = v_ref[b, :, vs].astype(BF16)
        o = _dot(pr["att"], vh) + _dot_nt(jnp.where(pr["sel"], seqs[b]["q_state"][:, sl], 0.0), s_scr[b, p])
        o = _rms(o, onorm) * _silu(r_ref[b, :, vs])
        o_ref[b, :, vs] = o.astype(o_ref.dtype)
        pr["upd"] = _dot_tn(vh, jnp.where(pr["sel"], seqs[b]["k_end"][:, sl], 0.0))
    for b in range(bb):
        for p in range(n_pairs):
            upd = [pr["upd"] for pr in probs if pr["b"] == b and pr["p"] == p]
            sl = slice(p * LANES, (p + 1) * LANES)
            s_scr[b, p] = s_scr[b, p] * jnp.exp(seqs[b]["g_end"][:, sl]) + sum(upd[1:], upd[0])

    @pl.when(c_idx == pl.num_programs(1) - 1)
    def _():
        st_ref[...] = s_scr[...]


def _gla_prompt(q, k, v, r, lg, onorm, bb):
    bsz, t, hdk = q.shape
    hdv = v.shape[2]
    dk, dv = hdk // H_A, hdv // H_A
    c = min(CHUNK, t)
    assert t % c == 0 and bsz % bb == 0 and LANES % dk == 0
    n_pairs = hdk // LANES
    blk = lambda n: pl.BlockSpec((bb, c, n), lambda i, j: (i, j, 0))
    o, st = pl.pallas_call(
        functools.partial(_gla_kernel, bb=bb, dk=dk, dv=dv),
        grid=(bsz // bb, t // c),
        in_specs=[blk(hdk), blk(hdk), blk(hdv), blk(hdv), blk(hdk), _const_spec(onorm.shape)],
        out_specs=[blk(hdv), pl.BlockSpec((bb, n_pairs, dv, LANES), lambda i, j: (i, 0, 0, 0))],
        out_shape=[jax.ShapeDtypeStruct((bsz, t, hdv), BF16), jax.ShapeDtypeStruct((bsz, n_pairs, dv, LANES), F32)],
        scratch_shapes=[pltpu.VMEM((bb, n_pairs, dv, LANES), F32)],
        compiler_params=_cparams("parallel", "arbitrary"), name="gla_prompt")(q, k, v, r, lg, onorm)
    hpp = LANES // dk
    s = st.reshape(bsz, n_pairs, dv, hpp, dk).transpose(0, 1, 3, 4, 2).reshape(bsz, H_A, dk, dv)
    return o, s


def _gdn_kernel(x_ref, z_ref, ab_ref, cw_ref, alog_ref, dtb_ref, on_ref, o_ref, s_ref, tail_ref, xbuf, s_scr,
                *, bb, n_heads):
    c_idx = pl.program_id(1)
    c = x_ref.shape[1]
    hd = z_ref.shape[2]
    dk = hd // n_heads
    pad = SUBLANES

    @pl.when(c_idx == 0)
    def _():
        s_scr[...] = jnp.zeros_like(s_scr)
        xbuf[:, 0:pad, :] = jnp.zeros((bb, pad, xbuf.shape[2]), F32)

    cp = c * n_heads
    t_p = lax.broadcasted_iota(jnp.int32, (c, cp), 0)
    s_p = lax.broadcasted_iota(jnp.int32, (c, cp), 1) % c
    h_p = lax.broadcasted_iota(jnp.int32, (c, cp), 1) // c
    incl_p = t_p >= s_p
    strict_p = t_p > s_p
    eye_p = t_p == s_p
    one16 = lambda m: jnp.where(m, 1.0, 0.0).astype(BF16)
    blk01 = [one16(h_p == h) for h in range(n_heads)]
    tri01 = one16(_tri_incl(c))
    ones_c = jnp.ones((c, c), BF16)
    e_row = lax.broadcasted_iota(jnp.int32, (LANES, cp), 0)
    e_head = lax.broadcasted_iota(jnp.int32, (LANES, cp), 1) // c
    expand_g = one16(e_row == e_head)
    expand_b = one16(e_row == e_head + n_heads)

    def blockdiag(x16, m01=None):
        return jnp.concatenate([x16 * (blk if m01 is None else blk * m01) for blk in blk01], axis=0)

    def lane_blocks(blocks, width):
        z = jnp.zeros_like(blocks[0])
        return jnp.concatenate([jnp.concatenate([blk if i == h else z for i in range(len(blocks))], axis=1)
                                for h, blk in enumerate(blocks)], axis=0)

    def sel3(x3, m01):
        d = lambda v: jnp.dot(v, m01, preferred_element_type=F32)
        return d(x3[0]) + (d(x3[1]) + d(x3[2]))

    cw = cw_ref[...]
    onorm = on_ref[...]
    neg_a = -jnp.exp(alog_ref[...])
    dtb = dtb_ref[...]
    seqs = []
    for b in range(bb):
        x = x_ref[b]
        xbuf[b, pad:pad + c, :] = x
        conv = x * cw[CONV_W - 1:CONV_W, :]
        for j in range(1, CONV_W):
            conv = conv + xbuf[b, pad - j:pad - j + c, :] * cw[CONV_W - 1 - j:CONV_W - j, :]
        xbuf[b, 0:pad, :] = x[c - pad:c, :]
        act = _silu(conv)
        ab = ab_ref[b]
        lgb = neg_a * _softplus(ab + dtb)
        beta = _sigmoid(ab)
        g = _mask_dot(tri01, lgb)
        qs, ks, vs, gcs, bcols = [], [], [], [], []
        for h in range(n_heads):
            qh = act[:, h * dk:(h + 1) * dk]
            kh = act[:, hd + h * dk:hd + (h + 1) * dk]
            qs.append(qh * lax.rsqrt(jnp.sum(qh * qh, axis=-1, keepdims=True) + EPS) * (dk ** -0.5))
            ks.append(kh * lax.rsqrt(jnp.sum(kh * kh, axis=-1, keepdims=True) + EPS))
            vs.append(act[:, 2 * hd + h * dk:2 * hd + (h + 1) * dk])
            gcs.append(g[:, h:h + 1])
            bcols.append(beta[:, n_heads + h:n_heads + h + 1])
        gcol_p = sel3(_split3(g), expand_g)
        bcol_p = sel3(_split3(beta), expand_b)
        grow_p = _mask_dot(ones_c, jnp.where(eye_p, gcol_p, 0.0))
        rel_p = jnp.where(incl_p, jnp.exp(jnp.minimum(gcol_p - grow_p, 0.0)), 0.0)
        k16 = [kh.astype(BF16) for kh in ks]
        kbd_t = lane_blocks(k16, dk)
        kk_p = _dot_nt(jnp.concatenate(k16, axis=1), kbd_t)
        qk_p = _dot_nt(jnp.concatenate(qs, axis=1), kbd_t) * rel_p
        a_p = jnp.where(strict_p, kk_p * rel_p, 0.0) * bcol_p
        seqs.append(dict(q=qs, k=ks, v=vs, gc=gcs, bcol=bcols, eg=[jnp.exp(gc) for gc in gcs],
                         g_end=[gc[c - 1:c, :] for gc in gcs], a=a_p, qk=qk_p.astype(BF16)))

    invs = [jnp.where(eye_p, 1.0, 0.0) - jnp.where((t_p // 2) == (s_p // 2), sq["a"], 0.0) for sq in seqs]
    a_splits = [_split2(sq["a"]) for sq in seqs]
    w = 2
    while w < c:
        lvl01 = one16(((t_p // (2 * w)) == (s_p // (2 * w))) & ((t_p // w) != (s_p // w)))
        inv_splits = [_split2(inv) for inv in invs]
        xs = [_dot_hi(isp, (blockdiag(ahi, lvl01), blockdiag(alo, lvl01)))
              for isp, (ahi, alo) in zip(inv_splits, a_splits)]
        invs = [inv - _dot_hi(x, (blockdiag(isp[0]), blockdiag(isp[1])))
                for inv, x, isp in zip(invs, xs, inv_splits)]
        w *= 2
    for sq, inv in zip(seqs, invs):
        isp = _split2(inv)
        sq["w"] = _dot_hi(isp, lane_blocks([k * (bc * eg) for k, bc, eg in zip(sq["k"], sq["bcol"], sq["eg"])], dk))
        sq["u"] = _dot_hi(isp, lane_blocks([v * bc for v, bc in zip(sq["v"], sq["bcol"])], dk))

    for b, sq in enumerate(seqs):
        sts = [s_scr[b, h] for h in range(n_heads)]
        sts16 = [st.astype(BF16) for st in sts]
        v_news = [sq["u"][:, h * dk:(h + 1) * dk] - _dot(sq["w"][:, h * dk:(h + 1) * dk], sts16[h])
                  for h in range(n_heads)]
        v_stack = jnp.concatenate(v_news, axis=0).astype(BF16)
        for h in range(n_heads):
            hs = slice(h * dk, (h + 1) * dk)
            o = (jnp.dot(sq["qk"] * blk01[h], v_stack, preferred_element_type=F32)
                 + _dot(sq["q"][h] * sq["eg"][h], sts16[h]))
            s_scr[b, h] = (sts[h] * jnp.exp(sq["g_end"][h])
                           + _dot_tn(sq["k"][h] * jnp.exp(sq["g_end"][h] - sq["gc"][h]), v_news[h]))
            o = _rms(o, onorm) * _silu(z_ref[b, :, hs])
            o_ref[b, :, hs] = o.astype(o_ref.dtype)

    @pl.when(c_idx == pl.num_programs(1) - 1)
    def _():
        s_ref[...] = s_scr[...]
        tail_ref[...] = xbuf[:, 0:pad, :]


def _gdn_prompt(qkv, z, ab, conv_w, a_log, dt_bias, onorm, bb):
    bsz, t, n3 = qkv.shape
    hd = z.shape[2]
    dk = hd // H_B
    c = min(CHUNK, t)
    assert t % c == 0 and bsz % bb == 0 and c >= 8
    blk = lambda n: pl.BlockSpec((bb, c, n), lambda i, j: (i, j, 0))
    o, s, tail = pl.pallas_call(
        functools.partial(_gdn_kernel, bb=bb, n_heads=H_B),
        grid=(bsz // bb, t // c),
        in_specs=[blk(n3), blk(hd), blk(LANES), _const_spec(conv_w.shape), _const_spec(a_log.shape),
                  _const_spec(dt_bias.shape), _const_spec(onorm.shape)],
        out_specs=[blk(hd), pl.BlockSpec((bb, H_B, dk, dk), lambda i, j: (i, 0, 0, 0)),
                   pl.BlockSpec((bb, SUBLANES, n3), lambda i, j: (i, 0, 0))],
        out_shape=[jax.ShapeDtypeStruct((bsz, t, hd), BF16), jax.ShapeDtypeStruct((bsz, H_B, dk, dk), F32),
                   jax.ShapeDtypeStruct((bsz, SUBLANES, n3), F32)],
        scratch_shapes=[pltpu.VMEM((bb, SUBLANES + c, n3), F32), pltpu.VMEM((bb, H_B, dk, dk), F32)],
        compiler_params=_cparams("parallel", "arbitrary"), name="gdn_prompt")(qkv, z, ab, conv_w, a_log, dt_bias, onorm)
    return o, s, tail[:, SUBLANES - (CONV_W - 1):, :]


def _sb_tiles(qm, kblk, vblks, bias2, u01, r_cols, mask):
    z = [jnp.dot(q, kblk, preferred_element_type=F32) + b for q, b in zip(qm, bias2)]
    sp = []
    for zz in z:
        s2 = jnp.maximum(zz, 0.0) + jnp.log2(1.0 + jnp.exp2(-jnp.abs(zz)))
        sp.append(s2 if mask is None else jnp.where(mask, s2, 0.0))
    inner = [jnp.dot(s2.astype(BF16), u01, preferred_element_type=F32) for s2 in sp]
    pv = None
    r_new = []
    for hh, (zz, s2, inn, r) in enumerate(zip(z, sp, inner, r_cols)):
        tot = s2 + inn
        w = jnp.exp2(zz - (tot + r))
        if mask is not None:
            w = jnp.where(mask, w, 0.0)
        r_new.append(r + tot[:, 0:1])
        d = _dot_nt(w, vblks[hh])
        pv = d if pv is None else pv + d
    return pv, r_new


def _sb_kernel(bias_ref, q_ref, k_ref, v_ref, o_ref, kb, vm, acc, rsum, *, hd, tk):
    p = pl.program_id(1)
    qi = pl.program_id(2)
    tq = q_ref.shape[1]
    t = k_ref.shape[2]
    hpp = LANES // hd
    row_head = lax.broadcasted_iota(jnp.int32, (LANES, tk), 0) // hd

    @pl.when(qi == 0)
    def _():
        for jj in range(t // tk):
            kb[jj] = k_ref[0, :, jj * tk:(jj + 1) * tk].astype(BF16)
            v = v_ref[0, :, jj * tk:(jj + 1) * tk]
            for hh in range(hpp):
                vm[hh, jj] = jnp.where(row_head == hh, v, 0.0).astype(BF16)

    lane_head_q = lax.broadcasted_iota(jnp.int32, (tq, LANES), 1) // hd
    qs = q_ref[0] * (hd ** -0.5 * LOG2E)
    qm = [jnp.where(lane_head_q == hh, qs, 0.0).astype(BF16) for hh in range(hpp)]
    bias2 = [bias_ref[p * hpp + hh] * LOG2E for hh in range(hpp)]
    rr = lax.broadcasted_iota(jnp.int32, (tk, tk), 0)
    ss = lax.broadcasted_iota(jnp.int32, (tk, tk), 1)
    u01 = jnp.where(rr > ss, 1.0, 0.0).astype(BF16)
    n_sub = tq // tk

    acc[...] = jnp.zeros_like(acc)
    rsum[...] = jnp.zeros_like(rsum)

    def tile(jj, mask, rows=slice(None)):
        vblks = [vm[hh, jj] for hh in range(hpp)]
        pv, r_new = _sb_tiles([q[rows] for q in qm], kb[jj], vblks, bias2, u01,
                              [rsum[hh, rows] for hh in range(hpp)], mask)
        acc[rows] += pv
        for hh in range(hpp):
            rsum[hh, rows] = r_new[hh]

    strict_lower = lax.broadcasted_iota(jnp.int32, (tk, tk), 1) < lax.broadcasted_iota(jnp.int32, (tk, tk), 0)
    for rb in range(n_sub):
        rows = slice(rb * tk, (rb + 1) * tk)
        tile(qi * n_sub + rb, strict_lower, rows)
        for sub in range(rb - 1, -1, -1):
            tile(qi * n_sub + sub, None, rows)

    def body(i, carry):
        for sub in range(n_sub):
            tile((qi - i) * n_sub - 1 - sub, None)
        return carry

    lax.fori_loop(0, qi, body, 0)
    o_ref[0] = acc[...].astype(o_ref.dtype)


def _sb_prompt(q, k_t, v_t, bias, tq=SB_TQ, tk=SB_TK):
    bsz, t, hhd = q.shape
    hd = hhd // H_C
    tq, tk = min(tq, t), min(tk, t)
    assert t % tq == 0 and tq % tk == 0 and LANES % hd == 0
    n_pairs = hhd // LANES
    hpp = LANES // hd
    return pl.pallas_call(
        functools.partial(_sb_kernel, hd=hd, tk=tk),
        grid=(bsz, n_pairs, t // tq),
        in_specs=[pl.BlockSpec(memory_space=pltpu.SMEM),
                  pl.BlockSpec((1, tq, LANES), lambda b, p, i: (b, i, p)),
                  pl.BlockSpec((1, LANES, t), lambda b, p, i: (b, p, 0)),
                  pl.BlockSpec((1, LANES, t), lambda b, p, i: (b, p, 0))],
        out_specs=pl.BlockSpec((1, tq, LANES), lambda b, p, i: (b, i, p)),
        out_shape=jax.ShapeDtypeStruct((bsz, t, hhd), BF16),
        scratch_shapes=[pltpu.VMEM((t // tk, LANES, tk), BF16), pltpu.VMEM((hpp, t // tk, LANES, tk), BF16),
                        pltpu.VMEM((tq, LANES), F32), pltpu.VMEM((hpp, tq, 1), F32)],
        compiler_params=_cparams("parallel", "parallel", "arbitrary"), name="sb_prompt")(bias, q, k_t, v_t)


def _rg_kernel(x_ref, y_ref, cw_ref, cb_ref, wa_ref, ba_ref, wx_ref, bx_ref, lam_ref,
               o_ref, h_ref, tail_ref, xbuf, a_scr, b_scr, h_scr):
    ti = pl.program_id(1)
    tt = x_ref.shape[1]
    d = x_ref.shape[2]
    pad = SUBLANES

    @pl.when(ti == 0)
    def _():
        h_scr[...] = jnp.zeros_like(h_scr)
        xbuf[0:pad, :] = jnp.zeros((pad, d), F32)

    x = x_ref[0]
    cw = cw_ref[...]
    xbuf[pad:pad + tt, :] = x
    conv = x * cw[CONV_W - 1:CONV_W, :]
    for j in range(1, CONV_W):
        conv = conv + xbuf[pad - j:pad - j + tt, :] * cw[CONV_W - 1 - j:CONV_W - j, :]
    xbuf[0:pad, :] = x[tt - pad:tt, :]
    xc = conv + cb_ref[...]
    nsp = _softplus(-lam_ref[...])
    xb = xc.astype(BF16)
    rg = _sigmoid(jnp.dot(xb, wa_ref[...], preferred_element_type=F32) + ba_ref[...])
    ig = _sigmoid(jnp.dot(xb, wx_ref[...], preferred_element_type=F32) + bx_ref[...])
    log_a = (-RG_C) * rg * nsp
    a = jnp.exp(log_a)
    bt = jnp.sqrt(_one_minus_exp2(log_a, a)) * (ig * xc)
    pos = ti * tt + lax.broadcasted_iota(jnp.int32, (tt, d), 0)
    a = jnp.where(pos == 0, 0.0, a)
    bt = jnp.where(pos == 0, ig * xc, bt)

    grp = SUBLANES
    row_in_grp = lax.broadcasted_iota(jnp.int32, (tt, d), 0) % grp
    sh = 1
    while sh < grp:
        a_prev = jnp.where(row_in_grp >= sh, pltpu.roll(a, sh, 0), 1.0)
        b_prev = jnp.where(row_in_grp >= sh, pltpu.roll(bt, sh, 0), 0.0)
        bt = a * b_prev + bt
        a = a * a_prev
        sh *= 2
    a_scr[...] = a
    b_scr[...] = bt

    def body(i, h):
        r0 = pl.multiple_of(i * grp, grp)
        hg = a_scr[pl.ds(r0, grp), :] * h + b_scr[pl.ds(r0, grp), :]
        b_scr[pl.ds(r0, grp), :] = hg
        return hg[grp - 1:grp, :]

    h_last = lax.fori_loop(0, tt // grp, body, h_scr[...])
    h_scr[...] = h_last
    hs = b_scr[...]
    o_ref[0] = (hs * _gelu_tanh(y_ref[0])).astype(o_ref.dtype)

    @pl.when(ti == pl.num_programs(1) - 1)
    def _():
        h_ref[0] = h_last
        tail_ref[0] = xbuf[0:pad, :]


def _rg_prompt(xr, yr, conv_w, conv_b, wa_bd, b_a, wx_bd, b_x, lam, tt=RG_TT):
    bsz, t, d = xr.shape
    tt = min(tt, t)
    assert t % tt == 0 and tt % SUBLANES == 0
    blk = pl.BlockSpec((1, tt, d), lambda b, i: (b, i, 0))
    consts = (conv_w, conv_b, wa_bd, b_a, wx_bd, b_x, lam)
    o, h, tail = pl.pallas_call(
        _rg_kernel,
        grid=(bsz, t // tt),
        in_specs=[blk, blk] + [_const_spec(a.shape) for a in consts],
        out_specs=[blk, pl.BlockSpec((1, 1, d), lambda b, i: (b, 0, 0)), pl.BlockSpec((1, SUBLANES, d), lambda b, i: (b, 0, 0))],
        out_shape=[jax.ShapeDtypeStruct((bsz, t, d), BF16), jax.ShapeDtypeStruct((bsz, 1, d), F32),
                   jax.ShapeDtypeStruct((bsz, SUBLANES, d), F32)],
        scratch_shapes=[pltpu.VMEM((SUBLANES + tt, d), F32), pltpu.VMEM((tt, d), F32), pltpu.VMEM((tt, d), F32),
                        pltpu.VMEM((1, d), F32)],
        compiler_params=_cparams("parallel", "arbitrary"), name="rglru_prompt")(xr, yr, *consts)
    return o, h[:, 0, :], tail[:, SUBLANES - (CONV_W - 1):, :]


def _even_prep_kernel(x_ref, hist_ref, cw_ref, ab_ref, alog_ref, dtb_ref, lgp_ref,
                      q_ref, k_ref, v_ref, nh_ref, gb_ref, lg_ref, *, n_heads):
    x = x_ref[...]
    cw = cw_ref[...]
    conv = x * cw[CONV_W - 1:CONV_W, :]
    for j in range(CONV_W - 1):
        conv = conv + hist_ref[j] * cw[j:j + 1, :]
    for j in range(CONV_W - 2):
        nh_ref[j] = hist_ref[j + 1]
    nh_ref[CONV_W - 2] = x
    act = _silu(conv)
    hd = q_ref.shape[1]
    dk = hd // n_heads
    for h in range(n_heads):
        hs = slice(h * dk, (h + 1) * dk)
        qh = act[:, hs]
        kh = act[:, hd + h * dk:hd + (h + 1) * dk]
        q_ref[:, hs] = qh * lax.rsqrt(jnp.sum(qh * qh, axis=-1, keepdims=True) + EPS) * (dk ** -0.5)
        k_ref[:, hs] = kh * lax.rsqrt(jnp.sum(kh * kh, axis=-1, keepdims=True) + EPS)
    v_ref[...] = act[:, 2 * hd:3 * hd]
    ab = ab_ref[...]
    lane = lax.broadcasted_iota(jnp.int32, ab.shape, 1)
    lgb = -jnp.exp(alog_ref[...]) * _softplus(ab + dtb_ref[...])
    gb_ref[...] = jnp.where(lane < n_heads, lgb, _sigmoid(ab))
    lg_ref[...] = -_softplus(-lgp_ref[...]) * (1.0 / GLA_TAU)


def _even_state_kernel(qa_ref, ka_ref, lga_ref, va_ref, qb_ref, kb_ref, vb_ref, gb_ref, sa_ref, sb_ref,
                       oa_ref, ob_ref, sa_out, sb_out, *, nb, dk_a, dv_a, dk_b):
    for j in range(nb):
        gbrow = gb_ref[j:j + 1, :]
        for h in range(H_A):
            rs = slice(h * dk_a, (h + 1) * dk_a)
            s_new = sa_ref[j, h] * jnp.exp(lga_ref[0, rs, j:j + 1]) + ka_ref[0, rs, j:j + 1] * va_ref[j:j + 1, h * dv_a:(h + 1) * dv_a]
            sa_out[j, h] = s_new
            oa_ref[j:j + 1, h * dv_a:(h + 1) * dv_a] = jnp.sum(qa_ref[0, rs, j:j + 1] * (dk_a ** -0.5) * s_new, axis=0, keepdims=True)
        for h in range(H_B):
            rs = slice(h * dk_b, (h + 1) * dk_b)
            st = sb_ref[j, h]
            eg = jnp.exp(gbrow[:, h:h + 1])
            beta = gbrow[:, H_B + h:H_B + h + 1]
            kcol = kb_ref[0, rs, j:j + 1]
            vrow = vb_ref[j:j + 1, rs]
            v_new = beta * (vrow - eg * jnp.sum(kcol * st, axis=0, keepdims=True))
            s_new = st * eg + kcol * v_new
            sb_out[j, h] = s_new
            ob_ref[j:j + 1, rs] = jnp.sum(qb_ref[0, rs, j:j + 1] * s_new, axis=0, keepdims=True)


def _even_out_kernel(oa_ref, ob_ref, r_ref, z_ref, ona_ref, onb_ref, o_ref, *, dv_a, dv_b):
    n_a = oa_ref.shape[1]
    for h in range(n_a // dv_a):
        sl = slice(h * dv_a, (h + 1) * dv_a)
        o_ref[:, sl] = (_rms(oa_ref[:, sl], ona_ref[...]) * _silu(r_ref[:, sl])).astype(o_ref.dtype)
    for h in range(ob_ref.shape[1] // dv_b):
        sl = slice(h * dv_b, (h + 1) * dv_b)
        o_ref[:, n_a + h * dv_b:n_a + (h + 1) * dv_b] = (_rms(ob_ref[:, sl], onb_ref[...]) * _silu(z_ref[:, sl])).astype(o_ref.dtype)


def _cols(x, nb):
    n, d = x.shape
    return x.reshape(n // nb, nb, d).transpose(0, 2, 1)


def _even_sample(qa, ka, va, ra, lgp, qkv, zb, ab, s_gla, s_gdn, hist, conv_w, a_log, dt_bias, on_a, on_b, nb=SAMPLE_SEQS):
    n, hdk_a = qa.shape
    hd_b = zb.shape[1]
    n3 = qkv.shape[1]
    hist_t = hist.transpose(1, 0, 2)
    full = lambda a: pl.BlockSpec(a.shape, lambda *_: (0,) * a.ndim)
    sds = lambda *s: jax.ShapeDtypeStruct(s, F32)
    ins = (qkv, hist_t, conv_w, ab, a_log, dt_bias, lgp)
    outs = [sds(n, hd_b), sds(n, hd_b), sds(n, hd_b), sds(CONV_W - 1, n, n3), sds(n, LANES), sds(n, hdk_a)]
    qb, kb, vb, new_hist_t, gb, lga = pl.pallas_call(
        functools.partial(_even_prep_kernel, n_heads=H_B),
        in_specs=[full(a) for a in ins], out_specs=[full(o) for o in outs], out_shape=outs,
        compiler_params=pltpu.CompilerParams(vmem_limit_bytes=VMEM_LIMIT_BYTES), name="even_sample_prep")(*ins)

    dk_a, dv_a, dk_b = hdk_a // H_A, va.shape[1] // H_A, hd_b // H_B
    assert n % nb == 0
    colspec = lambda d: pl.BlockSpec((1, d, nb), lambda i: (i, 0, 0))
    rowspec = lambda d: pl.BlockSpec((nb, d), lambda i: (i, 0))
    sa_spec = pl.BlockSpec((nb, H_A, dk_a, dv_a), lambda i: (i, 0, 0, 0))
    sb_spec = pl.BlockSpec((nb, H_B, dk_b, dk_b), lambda i: (i, 0, 0, 0))
    oa, ob, s_gla_new, s_gdn_new = pl.pallas_call(
        functools.partial(_even_state_kernel, nb=nb, dk_a=dk_a, dv_a=dv_a, dk_b=dk_b),
        grid=(n // nb,),
        in_specs=[colspec(hdk_a), colspec(hdk_a), colspec(hdk_a), rowspec(va.shape[1]),
                  colspec(hd_b), colspec(hd_b), rowspec(hd_b), rowspec(LANES), sa_spec, sb_spec],
        out_specs=[rowspec(va.shape[1]), rowspec(hd_b), sa_spec, sb_spec],
        out_shape=[sds(n, va.shape[1]), sds(n, hd_b), sds(*s_gla.shape), sds(*s_gdn.shape)],
        compiler_params=_cparams("parallel"), name="even_sample_state")(
            _cols(qa, nb), _cols(ka, nb), _cols(lga, nb), va, _cols(qb, nb), _cols(kb, nb), vb, gb, s_gla, s_gdn)

    ins = (oa, ob, ra, zb, on_a, on_b)
    out = jax.ShapeDtypeStruct((n, va.shape[1] + hd_b), BF16)
    o = pl.pallas_call(
        functools.partial(_even_out_kernel, dv_a=dv_a, dv_b=dk_b),
        in_specs=[full(a) for a in ins], out_specs=full(out), out_shape=out,
        compiler_params=pltpu.CompilerParams(vmem_limit_bytes=VMEM_LIMIT_BYTES), name="even_sample_out")(*ins)
    return o, s_gla_new, s_gdn_new, new_hist_t.transpose(1, 0, 2)


def _sb_decode_kernel(pt_ref, bias_ref, q_ref, *rest, n_pages):
    k_refs = rest[:n_pages]
    v_refs = rest[n_pages:2 * n_pages]
    o_ref = rest[2 * n_pages]
    hhd, page = k_refs[0].shape[1], k_refs[0].shape[2]
    n_heads = bias_ref.shape[0]
    hd = hhd // n_heads
    own = (lax.broadcasted_iota(jnp.int32, (n_heads, hhd), 0)
           == lax.broadcasted_iota(jnp.int32, (n_heads, hhd), 1) // hd)
    qrows = jnp.where(own, q_ref[0] * (hd ** -0.5 * LOG2E), 0.0).astype(BF16)
    rr = lax.broadcasted_iota(jnp.int32, (page, page), 0)
    ss = lax.broadcasted_iota(jnp.int32, (page, page), 1)
    u01 = jnp.where(rr > ss, 1.0, 0.0).astype(BF16)
    order = list(range(n_pages - 1, -1, -1))
    bias2 = jnp.concatenate([bias_ref[...] * LOG2E] * n_pages, axis=0)
    z_all = jnp.concatenate([_dot(qrows, k_refs[pg][0]) for pg in order], axis=0) + bias2
    sp = jnp.maximum(z_all, 0.0) + jnp.log2(1.0 + jnp.exp2(-jnp.abs(z_all)))
    tot = sp + jnp.dot(sp.astype(BF16), u01, preferred_element_type=F32)
    r_col = jnp.zeros((n_heads, 1), F32)
    ws = []
    for j in range(n_pages):
        rows = slice(j * n_heads, (j + 1) * n_heads)
        ws.append(jnp.exp2(z_all[rows] - (tot[rows] + r_col)))
        r_col = r_col + tot[rows, 0:1]
    acc = jnp.zeros((n_heads, hhd), F32)
    for w, pg in zip(ws, order):
        acc = acc + _dot_nt(w, v_refs[pg][0])
    o_ref[0] = jnp.sum(jnp.where(own, acc, 0.0), axis=0, keepdims=True)


def _sb_decode(q, cache_k, cache_v, page_table, bias):
    n, hhd = q.shape
    n_pool, page, n_heads, hd = cache_k.shape
    n_pages = page_table.shape[1]
    k_t = jnp.transpose(cache_k, (0, 2, 3, 1)).reshape(n_pool, hhd, page)
    v_t = jnp.transpose(cache_v, (0, 2, 3, 1)).reshape(n_pool, hhd, page)
    pt = page_table.reshape(-1)

    def page_spec(pg):
        return pl.BlockSpec((1, hhd, page), lambda b, pt_ref: (pt_ref[b * n_pages + pg], 0, 0))

    grid_spec = pltpu.PrefetchScalarGridSpec(
        num_scalar_prefetch=1, grid=(n,),
        in_specs=[pl.BlockSpec((n_heads, 1), lambda b, pt_ref: (0, 0)),
                  pl.BlockSpec((1, 1, hhd), lambda b, pt_ref: (b, 0, 0))]
                 + [page_spec(pg) for pg in range(n_pages)] * 2,
        out_specs=pl.BlockSpec((1, 1, hhd), lambda b, pt_ref: (b, 0, 0)))
    o = pl.pallas_call(
        functools.partial(_sb_decode_kernel, n_pages=n_pages),
        grid_spec=grid_spec, out_shape=jax.ShapeDtypeStruct((n, 1, hhd), F32),
        compiler_params=_cparams("parallel"), name="sb_decode")(
            pt, bias.reshape(n_heads, 1), q.reshape(n, 1, hhd), *([k_t] * n_pages), *([v_t] * n_pages))
    return o.reshape(n, hhd)


def _rg_decode_kernel(x_ref, y_ref, oc_ref, hist_ref, h0_ref, cw_ref, cb_ref, wa_ref, ba_ref, wx_ref, bx_ref, lam_ref,
                      o_ref, h_ref, nh_ref, *, reset):
    x = x_ref[...]
    cw = cw_ref[...]
    conv = x * cw[CONV_W - 1:CONV_W, :]
    for j in range(CONV_W - 1):
        conv = conv + hist_ref[j] * cw[j:j + 1, :]
    for j in range(CONV_W - 2):
        nh_ref[j] = hist_ref[j + 1]
    nh_ref[CONV_W - 2] = x
    xc = conv + cb_ref[...]
    xb = xc.astype(BF16)
    rg = _sigmoid(jnp.dot(xb, wa_ref[...], preferred_element_type=F32) + ba_ref[...])
    ig = _sigmoid(jnp.dot(xb, wx_ref[...], preferred_element_type=F32) + bx_ref[...])
    log_a = (-RG_C) * rg * _softplus(-lam_ref[...])
    if reset:
        h = ig * xc
    else:
        a = jnp.exp(log_a)
        h = a * h0_ref[...] + jnp.sqrt(_one_minus_exp2(log_a, a)) * (ig * xc)
    h_ref[...] = h
    d_c = oc_ref.shape[1]
    o_ref[:, 0:d_c] = oc_ref[...].astype(o_ref.dtype)
    o_ref[:, d_c:] = (h * _gelu_tanh(y_ref[...])).astype(o_ref.dtype)


def _odd_sample(oc, xr, yr, hist, h0, conv_w, conv_b, wa_bd, b_a, wx_bd, b_x, lam, offset):
    n, d = xr.shape
    hist_t = hist.transpose(1, 0, 2)
    full = lambda a: pl.BlockSpec(a.shape, lambda *_: (0,) * a.ndim)
    ins = (xr, yr, oc, hist_t, h0, conv_w, conv_b, wa_bd, b_a, wx_bd, b_x, lam)
    outs = [jax.ShapeDtypeStruct((n, oc.shape[1] + d), BF16), jax.ShapeDtypeStruct((n, d), F32),
            jax.ShapeDtypeStruct((CONV_W - 1, n, d), F32)]
    o, h, nh = pl.pallas_call(
        functools.partial(_rg_decode_kernel, reset=(offset == 0)),
        in_specs=[full(a) for a in ins], out_specs=[full(a) for a in outs], out_shape=outs,
        compiler_params=pltpu.CompilerParams(vmem_limit_bytes=VMEM_LIMIT_BYTES), name="rglru_decode")(*ins)
    return o, h, nh.transpose(1, 0, 2)


def _pad_cols(w, n):
    return jnp.pad(w, ((0, 0), (0, n - w.shape[1])))


def _block_diag(w):
    h, a, b = w.shape
    eye = jnp.eye(h, dtype=w.dtype)
    return (eye[:, None, :, None] * w[:, :, None, :]).reshape(h * a, h * b)


def _row(v, n=None):
    v = v.reshape(1, -1).astype(F32)
    return v if n is None else _pad_cols(v, n)


def kernel(x_prompt, x_sample, state_gla, state_gdn, state_gdn_conv, cache_sb_k, cache_sb_v, state_rg_h, state_rg_conv, page_table, ln_ffn, ffn_w_gate, ffn_w_up, ffn_w_down, ln_mix_e, w_in_e, gla_w_gate, gla_b_gate, gla_onorm, gdn_conv_w, gdn_a_log, gdn_dt_bias, gdn_onorm, w_out_e, ln_mix_o, w_in_o, sb_bias, rg_conv_w, rg_conv_b, rg_w_a, rg_b_a, rg_w_x, rg_b_x, rg_lambda, w_out_o, ln_final):
    bp, t, d = x_prompt.shape
    bs, ts, _ = x_sample.shape
    assert ts == 1
    depth = ln_ffn.shape[0]
    dk_a, dv_a = state_gla.shape[3], state_gla.shape[4]
    dk_b = state_gdn.shape[3]
    hd_c = cache_sb_k.shape[4]
    d_rnn = state_rg_h.shape[2]
    rank = gla_w_gate.shape[1]
    n_qa, n_va, n_b, n_c = H_A * dk_a, H_A * dv_a, H_B * dk_b, H_C * hd_c
    offset = page_table.shape[1] * cache_sb_k.shape[2]

    xp = x_prompt.reshape(bp * t, d)
    xs = x_sample.reshape(bs, d)
    outs = {}
    mix_p = mix_s = None
    for layer in range(depth):
        ffn = [(_row(ln_ffn[layer, i]), ffn_w_gate[layer, i].astype(BF16), ffn_w_up[layer, i].astype(BF16),
                ffn_w_down[layer, i].astype(BF16)) for i in range(2)]
        if layer % 2 == 0:
            e = layer // 2
            w = w_in_e[e]
            c0 = 0
            cols = []
            for n in (n_qa, n_qa, n_va, n_va, rank, 3 * n_b, n_b, 2 * H_B):
                cols.append(w[:, c0:c0 + n])
                c0 += n
            w_q, w_k, w_v, w_r, w_la, w_qkv, w_z, w_ab = cols
            w_abla = jnp.concatenate([_pad_cols(w_ab, LANES), _pad_cols(w_la, LANES)], axis=1)
            w_list = [a.astype(BF16) for a in (w_q, w_k, w_v, w_r, w_qkv, w_z, w_abla)]
            gate = (jnp.pad(gla_w_gate[e], ((0, LANES - rank), (0, 0))).astype(BF16), _row(gla_b_gate[e]))
            proj = (_row(ln_mix_e[e]), w_list)
            w_out = w_out_e[e].astype(BF16)
        else:
            o = layer // 2
            w = w_in_o[o]
            w_list = [w[:, :n_c].astype(BF16), w[:, 3 * n_c:3 * n_c + d_rnn].astype(BF16),
                      w[:, 3 * n_c + d_rnn:].astype(BF16)]
            wt_list = [w[:, n_c:2 * n_c].T.astype(BF16), w[:, 2 * n_c:3 * n_c].T.astype(BF16)]
            gate = None
            proj = (_row(ln_mix_o[o]), w_list)
            w_out = w_out_o[o].astype(BF16)

        if layer % 2 == 0:
            xp, *pp = _block_call(xp, None, ffn[0], proj=proj, gate=gate)
            xs, *ps = _block_call(xs, None, ffn[0], proj=proj, gate=gate)
        else:
            xp, *pp = _block_call(xp, None, ffn[0], proj=proj, tproj=(wt_list, bp))
            xs, *ps = _block_call(xs, None, ffn[0], proj=proj, tproj=(wt_list, 1))

        if layer % 2 == 0:
            e = layer // 2
            r3 = lambda a: a.reshape(bp, t, a.shape[1])
            qa, ka, va, ra, qkv, zb, ab, lgp = pp
            on_a, on_b = _row(gla_onorm[e]), _row(gdn_onorm[e])
            a_log, dtb = _row(gdn_a_log[e], LANES), _row(gdn_dt_bias[e], LANES)
            oa, s_gla_p = _gla_prompt(r3(qa), r3(ka), r3(va), r3(ra), r3(lgp), on_a, bb=bp)
            ob, s_gdn_p, tail_p = _gdn_prompt(r3(qkv), r3(zb), r3(ab), gdn_conv_w[e], a_log, dtb, on_b, bb=bp)
            mix_p = ([oa.reshape(bp * t, n_va), ob.reshape(bp * t, n_b)], [w_out[:n_va], w_out[n_va:]])
            qa, ka, va, ra, qkv, zb, ab, lgp = ps
            o_s, s_gla_s, s_gdn_s, hist_s = _even_sample(qa, ka, va, ra, lgp, qkv, zb, ab, state_gla[e], state_gdn[e],
                                                         state_gdn_conv[e], gdn_conv_w[e], a_log, dtb, on_a, on_b)
            mix_s = ([o_s], [w_out])
            outs.setdefault("gla_p", []).append(s_gla_p)
            outs.setdefault("gla_s", []).append(s_gla_s)
            outs.setdefault("gdn_p", []).append(s_gdn_p)
            outs.setdefault("gdn_s", []).append(s_gdn_s)
            outs.setdefault("gcv_p", []).append(tail_p)
            outs.setdefault("gcv_s", []).append(hist_s)
        else:
            o = layer // 2
            r3 = lambda a: a.reshape(bp, t, a.shape[1])
            to_cache = lambda a: a.reshape(a.shape[0], H_C, hd_c, a.shape[2]).transpose(0, 3, 1, 2)
            qc, xr, yr, kc_t, vc_t = pp
            wa_bd, wx_bd = _block_diag(rg_w_a[o]).astype(BF16), _block_diag(rg_w_x[o]).astype(BF16)
            rg_consts = (rg_conv_w[o], _row(rg_conv_b[o]), wa_bd, _row(rg_b_a[o]), wx_bd, _row(rg_b_x[o]), _row(rg_lambda[o]))
            oc = _sb_prompt(r3(qc), kc_t, vc_t, sb_bias[o])
            od, h_p, tail_p = _rg_prompt(r3(xr), r3(yr), *rg_consts)
            mix_p = ([oc.reshape(bp * t, n_c), od.reshape(bp * t, d_rnn)], [w_out[:n_c], w_out[n_c:]])
            outs.setdefault("sbk_p", []).append(to_cache(kc_t))
            outs.setdefault("sbv_p", []).append(to_cache(vc_t))
            outs.setdefault("rgh_p", []).append(h_p)
            outs.setdefault("rgc_p", []).append(tail_p)
            qc, xr, yr, kc_t, vc_t = ps
            oc_s = _sb_decode(qc, cache_sb_k[o], cache_sb_v[o], page_table, sb_bias[o])
            o_s, h_s, hist_s = _odd_sample(oc_s, xr, yr, state_rg_conv[o], state_rg_h[o], *rg_consts, offset)
            mix_s = ([o_s], [w_out])
            outs.setdefault("sbk_s", []).append(to_cache(kc_t).reshape(bs, 1, H_C, hd_c))
            outs.setdefault("sbv_s", []).append(to_cache(vc_t).reshape(bs, 1, H_C, hd_c))
            outs.setdefault("rgh_s", []).append(h_s)
            outs.setdefault("rgc_s", []).append(hist_s)

        final_g = _row(ln_final) if layer == depth - 1 else None
        (xp,) = _block_call(xp, mix_p, ffn[1], final_g=final_g, tm=TM_POST)
        (xs,) = _block_call(xs, mix_s, ffn[1], final_g=final_g)

    st = lambda name: jnp.stack(outs[name])
    return (xp.reshape(bp, t, d), xs.reshape(bs, 1, d), st("gla_p"), st("gla_s"), st("gdn_p"), st("gdn_s"),
            st("gcv_p"), st("gcv_s"), st("sbk_p"), st("sbk_s"), st("sbv_p"), st("sbv_s"),
            st("rgh_p"), st("rgh_s"), st("rgc_p"), st("rgc_s"))
```

```python
import functools
import math

import jax
import jax.numpy as jnp
from jax import lax
from jax.experimental import pallas as pl
from jax.experimental.pallas import tpu as pltpu

F32 = jnp.float32
BF16 = jnp.bfloat16

H_A = 4
GLA_TAU = 16.0
H_B = 4
CONV_W = 4
H_C = 8
H_D = 8
RG_C = 8.0
CHUNK = 64
EPS = 1e-6
LOG2E = 1.4426950408889634

VMEM_LIMIT_BYTES = 56 * 1024 * 1024
LANES = 128
SUBLANES = 8

TM_PROJ = 256
TM_POST = 512
SB_TQ, SB_TK = 512, 256
RG_TT = 512
SAMPLE_SEQS = 8


def _cparams(*sem):
    return pltpu.CompilerParams(dimension_semantics=sem, vmem_limit_bytes=VMEM_LIMIT_BYTES)


def _dot(a, b):
    return jnp.dot(a.astype(BF16), b.astype(BF16), preferred_element_type=F32)


def _dot_nt(a, b):
    return lax.dot_general(a.astype(BF16), b.astype(BF16), (((1,), (1,)), ((), ())), preferred_element_type=F32)


def _dot_tn(a, b):
    return lax.dot_general(a.astype(BF16), b.astype(BF16), (((0,), (0,)), ((), ())), preferred_element_type=F32)


def _split2(x):
    hi = x.astype(BF16)
    lo = (x - hi.astype(F32)).astype(BF16)
    return hi, lo


def _split3(x):
    x1 = x.astype(BF16)
    r1 = x - x1.astype(F32)
    x2 = r1.astype(BF16)
    x3 = (r1 - x2.astype(F32)).astype(BF16)
    return x1, x2, x3


def _dot_hi(a, b):
    a1, a2 = a if isinstance(a, tuple) else _split2(a)
    b1, b2 = b if isinstance(b, tuple) else _split2(b)
    d = lambda u, v: jnp.dot(u, v, preferred_element_type=F32)
    return d(a1, b1) + (d(a1, b2) + d(a2, b1))


def _mask_dot(m01, x):
    x1, x2, x3 = _split3(x)
    d = lambda v: jnp.dot(m01, v, preferred_element_type=F32)
    return d(x1) + (d(x2) + d(x3))


def _rms(x, g):
    return x * lax.rsqrt(jnp.mean(x * x, axis=-1, keepdims=True) + EPS) * g


def _sigmoid(x):
    return 1.0 / (1.0 + jnp.exp(-x))


def _silu(x):
    return x * _sigmoid(x)


def _softplus(x):
    return jnp.maximum(x, 0.0) + jnp.log1p(jnp.exp(-jnp.abs(x)))


def _gelu_tanh(x):
    return 0.5 * x * (1.0 + jnp.tanh(math.sqrt(2.0 / math.pi) * (x + 0.044715 * (x * x * x))))


def _one_minus_exp2(log_a, a):
    return jnp.tanh(-log_a) * (a * a + 1.0)


def _const_spec(shape):
    nd = len(shape)
    return pl.BlockSpec(shape, lambda *_: (0,) * nd, pipeline_mode=pl.Buffered(1))


def _block_kernel(*refs, n_mix, n_proj, n_tproj, has_gate, has_final, ff_chunk):
    it = iter(refs)
    x_ref = next(it)
    o_refs = [next(it) for _ in range(n_mix)]
    wo_refs = [next(it) for _ in range(n_mix)]
    gf_ref, wg_ref, wu_ref, wd_ref = next(it), next(it), next(it), next(it)
    if has_final:
        gfin_ref = next(it)
    if n_proj:
        gm_ref = next(it)
        wp_refs = [next(it) for _ in range(n_proj)]
    if has_gate:
        wgate_ref, bgate_ref = next(it), next(it)
    wt_refs = [next(it) for _ in range(n_tproj)]
    xo_ref = next(it)
    po_refs = [next(it) for _ in range(n_proj + int(has_gate))]
    tp_refs = [next(it) for _ in range(n_tproj)]

    x = x_ref[...]
    for o_ref, wo_ref in zip(o_refs, wo_refs):
        x = x + _dot(o_ref[...], wo_ref[...])
    h = _rms(x, gf_ref[...]).astype(BF16)
    d_ff = wg_ref.shape[1]
    acc = jnp.zeros_like(x)
    for c0 in range(0, d_ff, ff_chunk):
        gt = jnp.dot(h, wg_ref[:, c0:c0 + ff_chunk], preferred_element_type=F32)
        up = jnp.dot(h, wu_ref[:, c0:c0 + ff_chunk], preferred_element_type=F32)
        act = (_silu(gt) * up).astype(BF16)
        acc = acc + jnp.dot(act, wd_ref[c0:c0 + ff_chunk, :], preferred_element_type=F32)
    x = x + 0.5 * acc
    if has_final:
        xo_ref[...] = _rms(x, gfin_ref[...])
    else:
        xo_ref[...] = x
    if n_proj:
        hm = _rms(x, gm_ref[...]).astype(BF16)
        for j, wp_ref in enumerate(wp_refs):
            p = jnp.dot(hm, wp_ref[...], preferred_element_type=F32)
            if has_gate and j == n_proj - 1:
                po_refs[j][...] = p[:, :LANES]
                po_refs[j + 1][...] = _dot(p[:, LANES:], wgate_ref[...]) + bgate_ref[...]
            else:
                po_refs[j][...] = p
        for wt_ref, tp_ref in zip(wt_refs, tp_refs):
            tp_ref[0] = lax.dot_general(wt_ref[...], hm, (((1,), (1,)), ((), ())), preferred_element_type=F32)


def _block_call(x, mix, ffn, final_g=None, proj=None, gate=None, tproj=None, tm=TM_PROJ):
    m, d = x.shape
    tm = min(tm, m)
    assert m % tm == 0
    g_ffn, wg, wu, wd, ffn_idx = ffn
    d_ff = wg.shape[-1]
    ff_chunk = d_ff
    n_lead = len(ffn_idx)

    def stacked_spec(a):
        tail = a.shape[n_lead:]
        return pl.BlockSpec((None,) * n_lead + tail, lambda *_: tuple(ffn_idx) + (0,) * len(tail),
                            pipeline_mode=pl.Buffered(1))

    row = lambda n: pl.BlockSpec((tm, n), lambda i: (i, 0))
    args, specs = [x], [row(d)]
    o_list, wo_list = mix if mix is not None else ((), ())
    for o in o_list:
        args.append(o)
        specs.append(row(o.shape[1]))
    for w in wo_list:
        args.append(w)
        specs.append(_const_spec(w.shape))
    args.append(g_ffn)
    specs.append(_const_spec(g_ffn.shape))
    for a in (wg, wu, wd):
        args.append(a)
        specs.append(stacked_spec(a))
    if final_g is not None:
        args.append(final_g)
        specs.append(_const_spec(final_g.shape))
    out_shapes, out_specs = [jax.ShapeDtypeStruct((m, d), F32)], [row(d)]
    n_proj = 0
    if proj is not None:
        g_mix, w_list = proj
        n_proj = len(w_list)
        args.append(g_mix)
        specs.append(_const_spec(g_mix.shape))
        for w in w_list:
            args.append(w)
            specs.append(_const_spec(w.shape))
        widths = [w.shape[1] for w in w_list]
        if gate is not None:
            for a in gate:
                args.append(a)
                specs.append(_const_spec(a.shape))
            assert widths[-1] == 2 * LANES
            widths[-1:] = [LANES, gate[0].shape[1]]
        for n in widths:
            out_shapes.append(jax.ShapeDtypeStruct((m, n), F32))
            out_specs.append(row(n))
    n_tproj = 0
    if tproj is not None:
        assert proj is not None
        wt_list, bsz = tproj
        n_tproj = len(wt_list)
        t_rows = m // bsz
        assert t_rows % tm == 0
        nt = t_rows // tm
        for w in wt_list:
            args.append(w)
            specs.append(_const_spec(w.shape))
            out_shapes.append(jax.ShapeDtypeStruct((bsz, w.shape[0], t_rows), F32))
            out_specs.append(pl.BlockSpec((1, w.shape[0], tm), lambda i: (i // nt, 0, i % nt)))
    kern = functools.partial(_block_kernel, n_mix=len(o_list), n_proj=n_proj, n_tproj=n_tproj,
                             has_gate=gate is not None, has_final=final_g is not None, ff_chunk=ff_chunk)
    return pl.pallas_call(
        kern, grid=(m // tm,), in_specs=specs, out_specs=out_specs, out_shape=out_shapes,
        compiler_params=_cparams("parallel"), name="block")(*args)


def _tri_incl(c):
    r = lax.broadcasted_iota(jnp.int32, (c, c), 0)
    s = lax.broadcasted_iota(jnp.int32, (c, c), 1)
    return r >= s


def _gla_kernel(q_ref, k_ref, v_ref, r_ref, lg_ref, on_ref, o_ref, st_ref, s_scr, *, bb, dk, dv):
    c_idx = pl.program_id(1)
    c = q_ref.shape[1]
    n_pairs = q_ref.shape[2] // LANES
    hpp = LANES // dk

    @pl.when(c_idx == 0)
    def _():
        s_scr[...] = jnp.zeros_like(s_scr)

    incl = _tri_incl(c)
    tri01 = jnp.where(incl, 1.0, 0.0).astype(BF16)
    lane_head = lax.broadcasted_iota(jnp.int32, (c, LANES), 1) // dk
    onorm = on_ref[...]
    gs = [_mask_dot(tri01, -_softplus(-lg_ref[b]) * (1.0 / GLA_TAU)) for b in range(bb)]
    seqs = []
    for b, g in enumerate(gs):
        g_mid = g[c // 2 - 1:c // 2, :]
        g_end = g[c - 1:c, :]
        qs = q_ref[b] * (dk ** -0.5)
        kk = k_ref[b]
        seqs.append(dict(g_end=g_end,
                         q_state=qs * jnp.exp(g),
                         q_in=qs * jnp.exp(g - g_mid),
                         k_in=(kk * jnp.exp(g_mid - g)).astype(BF16),
                         k_end=kk * jnp.exp(g_end - g)))
    probs = []
    for b, sq in enumerate(seqs):
        for p in range(n_pairs):
            sl = slice(p * LANES, (p + 1) * LANES)
            for hh in range(hpp):
                sel = lane_head == hh
                att = _dot_nt(jnp.where(sel, sq["q_in"][:, sl], 0.0), sq["k_in"][:, sl])
                probs.append(dict(b=b, p=p, hh=hh, sl=sl, sel=sel, att=jnp.where(incl, att, 0.0).astype(BF16)))
    for pr in probs:
        b, p, hh, sl = pr["b"], pr["p"], pr["hh"], pr["sl"]
        vs = slice((p * hpp + hh) * dv, (p * hpp + hh + 1) * dv)
        vh = v_ref[b, :, vs].astype(BF16)
        o = _dot(pr["att"], vh) + _dot_nt(jnp.where(pr["sel"], seqs[b]["q_state"][:, sl], 0.0), s_scr[b, p])
        o = _rms(o, onorm) * _silu(r_ref[b, :, vs])
        o_ref[b, :, vs] = o.astype(o_ref.dtype)
        pr["upd"] = _dot_tn(vh, jnp.where(pr["sel"], seqs[b]["k_end"][:, sl], 0.0))
    for b in range(bb):
        for p in range(n_pairs):
            upd = [pr["upd"] for pr in probs if pr["b"] == b and pr["p"] == p]
            sl = slice(p * LANES, (p + 1) * LANES)
            s_scr[b, p] = s_scr[b, p] * jnp.exp(seqs[b]["g_end"][:, sl]) + sum(upd[1:], upd[0])

    @pl.when(c_idx == pl.num_programs(1) - 1)
    def _():
        st_ref[...] = s_scr[...]


def _gla_prompt(q, k, v, r, lg, onorm, bb):
    bsz, t, hdk = q.shape
    hdv = v.shape[2]
    dk, dv = hdk // H_A, hdv // H_A
    c = min(CHUNK, t)
    assert t % c == 0 and bsz % bb == 0 and LANES % dk == 0
    n_pairs = hdk // LANES
    blk = lambda n: pl.BlockSpec((bb, c, n), lambda i, j: (i, j, 0))
    o, st = pl.pallas_call(
        functools.partial(_gla_kernel, bb=bb, dk=dk, dv=dv),
        grid=(bsz // bb, t // c),
        in_specs=[blk(hdk), blk(hdk), blk(hdv), blk(hdv), blk(hdk), _const_spec(onorm.shape)],
        out_specs=[blk(hdv), pl.BlockSpec((bb, n_pairs, dv, LANES), lambda i, j: (i, 0, 0, 0))],
        out_shape=[jax.ShapeDtypeStruct((bsz, t, hdv), BF16), jax.ShapeDtypeStruct((bsz, n_pairs, dv, LANES), F32)],
        scratch_shapes=[pltpu.VMEM((bb, n_pairs, dv, LANES), F32)],
        compiler_params=_cparams("parallel", "arbitrary"), name="gla_prompt")(q, k, v, r, lg, onorm)
    hpp = LANES // dk
    s = st.reshape(bsz, n_pairs, dv, hpp, dk).transpose(0, 1, 3, 4, 2).reshape(bsz, H_A, dk, dv)
    return o, s


def _gdn_kernel(x_ref, z_ref, ab_ref, cw_ref, alog_ref, dtb_ref, on_ref, o_ref, s_ref, tail_ref, xbuf, s_scr,
                *, bb, n_heads):
    c_idx = pl.program_id(1)
    c = x_ref.shape[1]
    hd = z_ref.shape[2]
    dk = hd // n_heads
    pad = SUBLANES

    @pl.when(c_idx == 0)
    def _():
        s_scr[...] = jnp.zeros_like(s_scr)
        xbuf[:, 0:pad, :] = jnp.zeros((bb, pad, xbuf.shape[2]), F32)

    cp = c * n_heads
    t_p = lax.broadcasted_iota(jnp.int32, (c, cp), 0)
    s_p = lax.broadcasted_iota(jnp.int32, (c, cp), 1) % c
    h_p = lax.broadcasted_iota(jnp.int32, (c, cp), 1) // c
    incl_p = t_p >= s_p
    strict_p = t_p > s_p
    eye_p = t_p == s_p
    one16 = lambda m: jnp.where(m, 1.0, 0.0).astype(BF16)
    blk01 = [one16(h_p == h) for h in range(n_heads)]
    tri01 = one16(_tri_incl(c))
    ones_c = jnp.ones((c, c), BF16)
    e_row = lax.broadcasted_iota(jnp.int32, (LANES, cp), 0)
    e_head = lax.broadcasted_iota(jnp.int32, (LANES, cp), 1) // c
    expand_g = one16(e_row == e_head)
    expand_b = one16(e_row == e_head + n_heads)

    def blockdiag(x16, m01=None):
        return jnp.concatenate([x16 * (blk if m01 is None else blk * m01) for blk in blk01], axis=0)

    def lane_blocks(blocks, width):
        z = jnp.zeros_like(blocks[0])
        return jnp.concatenate([jnp.concatenate([blk if i == h else z for i in range(len(blocks))], axis=1)
                                for h, blk in enumerate(blocks)], axis=0)

    def sel3(x3, m01):
        d = lambda v: jnp.dot(v, m01, preferred_element_type=F32)
        return d(x3[0]) + (d(x3[1]) + d(x3[2]))

    cw = cw_ref[...]
    onorm = on_ref[...]
    neg_a = -jnp.exp(alog_ref[...])
    dtb = dtb_ref[...]
    seqs = []
    for b in range(bb):
        x = x_ref[b]
        xbuf[b, pad:pad + c, :] = x
        conv = x * cw[CONV_W - 1:CONV_W, :]
        for j in range(1, CONV_W):
            conv = conv + xbuf[b, pad - j:pad - j + c, :] * cw[CONV_W - 1 - j:CONV_W - j, :]
        xbuf[b, 0:pad, :] = x[c - pad:c, :]
        act = _silu(conv)
        ab = ab_ref[b]
        lgb = neg_a * _softplus(ab + dtb)
        beta = _sigmoid(ab)
        g = _mask_dot(tri01, lgb)
        qs, ks, vs, gcs, bcols = [], [], [], [], []
        for h in range(n_heads):
            qh = act[:, h * dk:(h + 1) * dk]
            kh = act[:, hd + h * dk:hd + (h + 1) * dk]
            qs.append(qh * lax.rsqrt(jnp.sum(qh * qh, axis=-1, keepdims=True) + EPS) * (dk ** -0.5))
            ks.append(kh * lax.rsqrt(jnp.sum(kh * kh, axis=-1, keepdims=True) + EPS))
            vs.append(act[:, 2 * hd + h * dk:2 * hd + (h + 1) * dk])
            gcs.append(g[:, h:h + 1])
            bcols.append(beta[:, n_heads + h:n_heads + h + 1])
        gcol_p = sel3(_split3(g), expand_g)
        bcol_p = sel3(_split3(beta), expand_b)
        grow_p = _mask_dot(ones_c, jnp.where(eye_p, gcol_p, 0.0))
        rel_p = jnp.where(incl_p, jnp.exp(jnp.minimum(gcol_p - grow_p, 0.0)), 0.0)
        k16 = [kh.astype(BF16) for kh in ks]
        kbd_t = lane_blocks(k16, dk)
        kk_p = _dot_nt(jnp.concatenate(k16, axis=1), kbd_t)
        qk_p = _dot_nt(jnp.concatenate(qs, axis=1), kbd_t) * rel_p
        a_p = jnp.where(strict_p, kk_p * rel_p, 0.0) * bcol_p
        seqs.append(dict(q=qs, k=ks, v=vs, gc=gcs, bcol=bcols, eg=[jnp.exp(gc) for gc in gcs],
                         g_end=[gc[c - 1:c, :] for gc in gcs], a=a_p, qk=qk_p.astype(BF16)))

    invs = [jnp.where(eye_p, 1.0, 0.0) - jnp.where((t_p // 2) == (s_p // 2), sq["a"], 0.0) for sq in seqs]
    a_splits = [_split2(sq["a"]) for sq in seqs]
    w = 2
    while w < c:
        lvl01 = one16(((t_p // (2 * w)) == (s_p // (2 * w))) & ((t_p // w) != (s_p // w)))
        inv_splits = [_split2(inv) for inv in invs]
        xs = [_dot_hi(isp, (blockdiag(ahi, lvl01), blockdiag(alo, lvl01)))
              for isp, (ahi, alo) in zip(inv_splits, a_splits)]
        invs = [inv - _dot_hi(x, (blockdiag(isp[0]), blockdiag(isp[1])))
                for inv, x, isp in zip(invs, xs, inv_splits)]
        w *= 2
    for sq, inv in zip(seqs, invs):
        isp = _split2(inv)
        sq["w"] = _dot_hi(isp, lane_blocks([k * (bc * eg) for k, bc, eg in zip(sq["k"], sq["bcol"], sq["eg"])], dk))
        sq["u"] = _dot_hi(isp, lane_blocks([v * bc for v, bc in zip(sq["v"], sq["bcol"])], dk))

    for b, sq in enumerate(seqs):
        sts = [s_scr[b, h] for h in range(n_heads)]
        sts16 = [st.astype(BF16) for st in sts]
        v_news = [sq["u"][:, h * dk:(h + 1) * dk] - _dot(sq["w"][:, h * dk:(h + 1) * dk], sts16[h])
                  for h in range(n_heads)]
        v_stack = jnp.concatenate(v_news, axis=0).astype(BF16)
        for h in range(n_heads):
            hs = slice(h * dk, (h + 1) * dk)
            o = (jnp.dot(sq["qk"] * blk01[h], v_stack, preferred_element_type=F32)
                 + _dot(sq["q"][h] * sq["eg"][h], sts16[h]))
            s_scr[b, h] = (sts[h] * jnp.exp(sq["g_end"][h])
                           + _dot_tn(sq["k"][h] * jnp.exp(sq["g_end"][h] - sq["gc"][h]), v_news[h]))
            o = _rms(o, onorm) * _silu(z_ref[b, :, hs])
            o_ref[b, :, hs] = o.astype(o_ref.dtype)

    @pl.when(c_idx == pl.num_programs(1) - 1)
    def _():
        s_ref[...] = s_scr[...]
        tail_ref[...] = xbuf[:, 0:pad, :]


def _gdn_prompt(qkv, z, ab, conv_w, a_log, dt_bias, onorm, bb):
    bsz, t, n3 = qkv.shape
    hd = z.shape[2]
    dk = hd // H_B
    c = min(CHUNK, t)
    assert t % c == 0 and bsz % bb == 0 and c >= 8
    blk = lambda n: pl.BlockSpec((bb, c, n), lambda i, j: (i, j, 0))
    o, s, tail = pl.pallas_call(
        functools.partial(_gdn_kernel, bb=bb, n_heads=H_B),
        grid=(bsz // bb, t // c),
        in_specs=[blk(n3), blk(hd), blk(LANES), _const_spec(conv_w.shape), _const_spec(a_log.shape),
                  _const_spec(dt_bias.shape), _const_spec(onorm.shape)],
        out_specs=[blk(hd), pl.BlockSpec((bb, H_B, dk, dk), lambda i, j: (i, 0, 0, 0)),
                   pl.BlockSpec((bb, SUBLANES, n3), lambda i, j: (i, 0, 0))],
        out_shape=[jax.ShapeDtypeStruct((bsz, t, hd), BF16), jax.ShapeDtypeStruct((bsz, H_B, dk, dk), F32),
                   jax.ShapeDtypeStruct((bsz, SUBLANES, n3), F32)],
        scratch_shapes=[pltpu.VMEM((bb, SUBLANES + c, n3), F32), pltpu.VMEM((bb, H_B, dk, dk), F32)],
        compiler_params=_cparams("parallel", "arbitrary"), name="gdn_prompt")(qkv, z, ab, conv_w, a_log, dt_bias, onorm)
    return o, s, tail[:, SUBLANES - (CONV_W - 1):, :]


def _sb_tiles(qm, kblk, vblks, bias2, u01, r_cols, mask):
    z = [jnp.dot(q, kblk, preferred_element_type=F32) + b for q, b in zip(qm, bias2)]
    sp = []
    for zz in z:
        s2 = jnp.maximum(zz, 0.0) + jnp.log2(1.0 + jnp.exp2(-jnp.abs(zz)))
        sp.append(s2 if mask is None else jnp.where(mask, s2, 0.0))
    inner = [jnp.dot(s2.astype(BF16), u01, preferred_element_type=F32) for s2 in sp]
    pv = None
    r_new = []
    for hh, (zz, s2, inn, r) in enumerate(zip(z, sp, inner, r_cols)):
        tot = s2 + inn
        w = jnp.exp2(zz - (tot + r))
        if mask is not None:
            w = jnp.where(mask, w, 0.0)
        r_new.append(r + tot[:, 0:1])
        d = _dot_nt(w, vblks[hh])
        pv = d if pv is None else pv + d
    return pv, r_new


def _sb_kernel(bias_ref, q_ref, k_ref, v_ref, o_ref, kb, vm, acc, rsum, *, hd, tk):
    p = pl.program_id(1)
    qi = pl.program_id(2)
    tq = q_ref.shape[1]
    t = k_ref.shape[2]
    hpp = LANES // hd
    row_head = lax.broadcasted_iota(jnp.int32, (LANES, tk), 0) // hd

    @pl.when(qi == 0)
    def _():
        for jj in range(t // tk):
            kb[jj] = k_ref[0, :, jj * tk:(jj + 1) * tk].astype(BF16)
            v = v_ref[0, :, jj * tk:(jj + 1) * tk]
            for hh in range(hpp):
                vm[hh, jj] = jnp.where(row_head == hh, v, 0.0).astype(BF16)

    lane_head_q = lax.broadcasted_iota(jnp.int32, (tq, LANES), 1) // hd
    qs = q_ref[0] * (hd ** -0.5 * LOG2E)
    qm = [jnp.where(lane_head_q == hh, qs, 0.0).astype(BF16) for hh in range(hpp)]
    bias2 = [bias_ref[p * hpp + hh] * LOG2E for hh in range(hpp)]
    rr = lax.broadcasted_iota(jnp.int32, (tk, tk), 0)
    ss = lax.broadcasted_iota(jnp.int32, (tk, tk), 1)
    u01 = jnp.where(rr > ss, 1.0, 0.0).astype(BF16)
    n_sub = tq // tk

    acc[...] = jnp.zeros_like(acc)
    rsum[...] = jnp.zeros_like(rsum)

    def tile(jj, mask, rows=slice(None)):
        vblks = [vm[hh, jj] for hh in range(hpp)]
        pv, r_new = _sb_tiles([q[rows] for q in qm], kb[jj], vblks, bias2, u01,
                              [rsum[hh, rows] for hh in range(hpp)], mask)
        acc[rows] += pv
        for hh in range(hpp):
            rsum[hh, rows] = r_new[hh]

    strict_lower = lax.broadcasted_iota(jnp.int32, (tk, tk), 1) < lax.broadcasted_iota(jnp.int32, (tk, tk), 0)
    for rb in range(n_sub):
        rows = slice(rb * tk, (rb + 1) * tk)
        tile(qi * n_sub + rb, strict_lower, rows)
        for sub in range(rb - 1, -1, -1):
            tile(qi * n_sub + sub, None, rows)

    n_vis = qi * n_sub
    unroll = 2 * n_sub
    rem = n_vis % unroll

    @pl.when(rem != 0)
    def _():
        for sub in range(n_sub):
            tile(n_vis - 1 - sub, None)

    def body(i, carry):
        for sub in range(unroll):
            tile(n_vis - rem - 1 - i * unroll - sub, None)
        return carry

    lax.fori_loop(0, n_vis // unroll, body, 0)
    o_ref[0] = acc[...].astype(o_ref.dtype)


def _sb_prompt(q, k_t, v_t, bias, tq=SB_TQ, tk=SB_TK):
    bsz, t, hhd = q.shape
    hd = hhd // H_C
    tq, tk = min(tq, t), min(tk, t)
    assert t % tq == 0 and tq % tk == 0 and LANES % hd == 0
    n_pairs = hhd // LANES
    hpp = LANES // hd
    return pl.pallas_call(
        functools.partial(_sb_kernel, hd=hd, tk=tk),
        grid=(bsz, n_pairs, t // tq),
        in_specs=[pl.BlockSpec(memory_space=pltpu.SMEM),
                  pl.BlockSpec((1, tq, LANES), lambda b, p, i: (b, i, p)),
                  pl.BlockSpec((1, LANES, t), lambda b, p, i: (b, p, 0)),
                  pl.BlockSpec((1, LANES, t), lambda b, p, i: (b, p, 0))],
        out_specs=pl.BlockSpec((1, tq, LANES), lambda b, p, i: (b, i, p)),
        out_shape=jax.ShapeDtypeStruct((bsz, t, hhd), BF16),
        scratch_shapes=[pltpu.VMEM((t // tk, LANES, tk), BF16), pltpu.VMEM((hpp, t // tk, LANES, tk), BF16),
                        pltpu.VMEM((tq, LANES), F32), pltpu.VMEM((hpp, tq, 1), F32)],
        compiler_params=_cparams("parallel", "parallel", "arbitrary"), name="sb_prompt")(bias, q, k_t, v_t)


def _rg_kernel(x_ref, y_ref, cw_ref, cb_ref, wa_ref, ba_ref, wx_ref, bx_ref, lam_ref,
               o_ref, h_ref, tail_ref, xbuf, a_scr, b_scr, h_scr):
    ti = pl.program_id(1)
    tt = x_ref.shape[1]
    d = x_ref.shape[2]
    pad = SUBLANES

    @pl.when(ti == 0)
    def _():
        h_scr[...] = jnp.zeros_like(h_scr)
        xbuf[0:pad, :] = jnp.zeros((pad, d), F32)

    x = x_ref[0]
    cw = cw_ref[...]
    xbuf[pad:pad + tt, :] = x
    conv = x * cw[CONV_W - 1:CONV_W, :]
    for j in range(1, CONV_W):
        conv = conv + xbuf[pad - j:pad - j + tt, :] * cw[CONV_W - 1 - j:CONV_W - j, :]
    xbuf[0:pad, :] = x[tt - pad:tt, :]
    xc = conv + cb_ref[...]
    nsp = _softplus(-lam_ref[...])
    xb = xc.astype(BF16)
    rg = _sigmoid(jnp.dot(xb, wa_ref[...], preferred_element_type=F32) + ba_ref[...])
    ig = _sigmoid(jnp.dot(xb, wx_ref[...], preferred_element_type=F32) + bx_ref[...])
    log_a = (-RG_C) * rg * nsp
    a = jnp.exp(log_a)
    bt = jnp.sqrt(_one_minus_exp2(log_a, a)) * (ig * xc)
    pos = ti * tt + lax.broadcasted_iota(jnp.int32, (tt, d), 0)
    a = jnp.where(pos == 0, 0.0, a)
    bt = jnp.where(pos == 0, ig * xc, bt)

    grp = SUBLANES
    row_in_grp = lax.broadcasted_iota(jnp.int32, (tt, d), 0) % grp
    sh = 1
    while sh < grp:
        a_prev = jnp.where(row_in_grp >= sh, pltpu.roll(a, sh, 0), 1.0)
        b_prev = jnp.where(row_in_grp >= sh, pltpu.roll(bt, sh, 0), 0.0)
        bt = a * b_prev + bt
        a = a * a_prev
        sh *= 2
    a_scr[...] = a
    b_scr[...] = bt

    def body(i, h):
        r0 = pl.multiple_of(i * grp, grp)
        hg = a_scr[pl.ds(r0, grp), :] * h + b_scr[pl.ds(r0, grp), :]
        b_scr[pl.ds(r0, grp), :] = hg
        return hg[grp - 1:grp, :]

    h_last = lax.fori_loop(0, tt // grp, body, h_scr[...])
    h_scr[...] = h_last
    hs = b_scr[...]
    o_ref[0] = (hs * _gelu_tanh(y_ref[0])).astype(o_ref.dtype)

    @pl.when(ti == pl.num_programs(1) - 1)
    def _():
        h_ref[0] = h_last
        tail_ref[0] = xbuf[0:pad, :]


def _rg_prompt(xr, yr, conv_w, conv_b, wa_bd, b_a, wx_bd, b_x, lam, tt=RG_TT):
    bsz, t, d = xr.shape
    tt = min(tt, t)
    assert t % tt == 0 and tt % SUBLANES == 0
    blk = pl.BlockSpec((1, tt, d), lambda b, i: (b, i, 0))
    consts = (conv_w, conv_b, wa_bd, b_a, wx_bd, b_x, lam)
    o, h, tail = pl.pallas_call(
        _rg_kernel,
        grid=(bsz, t // tt),
        in_specs=[blk, blk] + [_const_spec(a.shape) for a in consts],
        out_specs=[blk, pl.BlockSpec((1, 1, d), lambda b, i: (b, 0, 0)), pl.BlockSpec((1, SUBLANES, d), lambda b, i: (b, 0, 0))],
        out_shape=[jax.ShapeDtypeStruct((bsz, t, d), BF16), jax.ShapeDtypeStruct((bsz, 1, d), F32),
                   jax.ShapeDtypeStruct((bsz, SUBLANES, d), F32)],
        scratch_shapes=[pltpu.VMEM((SUBLANES + tt, d), F32), pltpu.VMEM((tt, d), F32), pltpu.VMEM((tt, d), F32),
                        pltpu.VMEM((1, d), F32)],
        compiler_params=_cparams("parallel", "arbitrary"), name="rglru_prompt")(xr, yr, *consts)
    return o, h[:, 0, :], tail[:, SUBLANES - (CONV_W - 1):, :]


def _even_prep_kernel(x_ref, hist_ref, cw_ref, ab_ref, alog_ref, dtb_ref, lgp_ref,
                      q_ref, k_ref, v_ref, nh_ref, gb_ref, lg_ref, *, n_heads):
    x = x_ref[...]
    cw = cw_ref[...]
    conv = x * cw[CONV_W - 1:CONV_W, :]
    for j in range(CONV_W - 1):
        conv = conv + hist_ref[j] * cw[j:j + 1, :]
    for j in range(CONV_W - 2):
        nh_ref[j] = hist_ref[j + 1]
    nh_ref[CONV_W - 2] = x
    act = _silu(conv)
    hd = q_ref.shape[1]
    dk = hd // n_heads
    for h in range(n_heads):
        hs = slice(h * dk, (h + 1) * dk)
        qh = act[:, hs]
        kh = act[:, hd + h * dk:hd + (h + 1) * dk]
        q_ref[:, hs] = qh * lax.rsqrt(jnp.sum(qh * qh, axis=-1, keepdims=True) + EPS) * (dk ** -0.5)
        k_ref[:, hs] = kh * lax.rsqrt(jnp.sum(kh * kh, axis=-1, keepdims=True) + EPS)
    v_ref[...] = act[:, 2 * hd:3 * hd]
    ab = ab_ref[...]
    lane = lax.broadcasted_iota(jnp.int32, ab.shape, 1)
    lgb = -jnp.exp(alog_ref[...]) * _softplus(ab + dtb_ref[...])
    gb_ref[...] = jnp.where(lane < n_heads, lgb, _sigmoid(ab))
    lg_ref[...] = -_softplus(-lgp_ref[...]) * (1.0 / GLA_TAU)


def _even_state_kernel(qa_ref, ka_ref, lga_ref, va_ref, qb_ref, kb_ref, vb_ref, gb_ref, sa_ref, sb_ref,
                       oa_ref, ob_ref, sa_out, sb_out, *, nb, dk_a, dv_a, dk_b):
    for j in range(nb):
        gbrow = gb_ref[j:j + 1, :]
        for h in range(H_A):
            rs = slice(h * dk_a, (h + 1) * dk_a)
            s_new = sa_ref[j, h] * jnp.exp(lga_ref[0, rs, j:j + 1]) + ka_ref[0, rs, j:j + 1] * va_ref[j:j + 1, h * dv_a:(h + 1) * dv_a]
            sa_out[j, h] = s_new
            oa_ref[j:j + 1, h * dv_a:(h + 1) * dv_a] = jnp.sum(qa_ref[0, rs, j:j + 1] * (dk_a ** -0.5) * s_new, axis=0, keepdims=True)
        for h in range(H_B):
            rs = slice(h * dk_b, (h + 1) * dk_b)
            st = sb_ref[j, h]
            eg = jnp.exp(gbrow[:, h:h + 1])
            beta = gbrow[:, H_B + h:H_B + h + 1]
            kcol = kb_ref[0, rs, j:j + 1]
            vrow = vb_ref[j:j + 1, rs]
            v_new = beta * (vrow - eg * jnp.sum(kcol * st, axis=0, keepdims=True))
            s_new = st * eg + kcol * v_new
            sb_out[j, h] = s_new
            ob_ref[j:j + 1, rs] = jnp.sum(qb_ref[0, rs, j:j + 1] * s_new, axis=0, keepdims=True)


def _even_out_kernel(oa_ref, ob_ref, r_ref, z_ref, ona_ref, onb_ref, o_ref, *, dv_a, dv_b):
    n_a = oa_ref.shape[1]
    for h in range(n_a // dv_a):
        sl = slice(h * dv_a, (h + 1) * dv_a)
        o_ref[:, sl] = (_rms(oa_ref[:, sl], ona_ref[...]) * _silu(r_ref[:, sl])).astype(o_ref.dtype)
    for h in range(ob_ref.shape[1] // dv_b):
        sl = slice(h * dv_b, (h + 1) * dv_b)
        o_ref[:, n_a + h * dv_b:n_a + (h + 1) * dv_b] = (_rms(ob_ref[:, sl], onb_ref[...]) * _silu(z_ref[:, sl])).astype(o_ref.dtype)


def _cols(x, nb):
    n, d = x.shape
    return x.reshape(n // nb, nb, d).transpose(0, 2, 1)


def _even_sample(qa, ka, va, ra, lgp, qkv, zb, ab, s_gla, s_gdn, hist, conv_w, a_log, dt_bias, on_a, on_b, nb=SAMPLE_SEQS):
    n, hdk_a = qa.shape
    hd_b = zb.shape[1]
    n3 = qkv.shape[1]
    hist_t = hist.transpose(1, 0, 2)
    full = lambda a: pl.BlockSpec(a.shape, lambda *_: (0,) * a.ndim)
    sds = lambda *s: jax.ShapeDtypeStruct(s, F32)
    ins = (qkv, hist_t, conv_w, ab, a_log, dt_bias, lgp)
    outs = [sds(n, hd_b), sds(n, hd_b), sds(n, hd_b), sds(CONV_W - 1, n, n3), sds(n, LANES), sds(n, hdk_a)]
    qb, kb, vb, new_hist_t, gb, lga = pl.pallas_call(
        functools.partial(_even_prep_kernel, n_heads=H_B),
        in_specs=[full(a) for a in ins], out_specs=[full(o) for o in outs], out_shape=outs,
        compiler_params=pltpu.CompilerParams(vmem_limit_bytes=VMEM_LIMIT_BYTES), name="even_sample_prep")(*ins)

    dk_a, dv_a, dk_b = hdk_a // H_A, va.shape[1] // H_A, hd_b // H_B
    assert n % nb == 0
    colspec = lambda d: pl.BlockSpec((1, d, nb), lambda i: (i, 0, 0))
    rowspec = lambda d: pl.BlockSpec((nb, d), lambda i: (i, 0))
    sa_spec = pl.BlockSpec((nb, H_A, dk_a, dv_a), lambda i: (i, 0, 0, 0))
    sb_spec = pl.BlockSpec((nb, H_B, dk_b, dk_b), lambda i: (i, 0, 0, 0))
    oa, ob, s_gla_new, s_gdn_new = pl.pallas_call(
        functools.partial(_even_state_kernel, nb=nb, dk_a=dk_a, dv_a=dv_a, dk_b=dk_b),
        grid=(n // nb,),
        in_specs=[colspec(hdk_a), colspec(hdk_a), colspec(hdk_a), rowspec(va.shape[1]),
                  colspec(hd_b), colspec(hd_b), rowspec(hd_b), rowspec(LANES), sa_spec, sb_spec],
        out_specs=[rowspec(va.shape[1]), rowspec(hd_b), sa_spec, sb_spec],
        out_shape=[sds(n, va.shape[1]), sds(n, hd_b), sds(*s_gla.shape), sds(*s_gdn.shape)],
        compiler_params=_cparams("parallel"), name="even_sample_state")(
            _cols(qa, nb), _cols(ka, nb), _cols(lga, nb), va, _cols(qb, nb), _cols(kb, nb), vb, gb, s_gla, s_gdn)

    ins = (oa, ob, ra, zb, on_a, on_b)
    out = jax.ShapeDtypeStruct((n, va.shape[1] + hd_b), BF16)
    o = pl.pallas_call(
        functools.partial(_even_out_kernel, dv_a=dv_a, dv_b=dk_b),
        in_specs=[full(a) for a in ins], out_specs=full(out), out_shape=out,
        compiler_params=pltpu.CompilerParams(vmem_limit_bytes=VMEM_LIMIT_BYTES), name="even_sample_out")(*ins)
    return o, s_gla_new, s_gdn_new, new_hist_t.transpose(1, 0, 2)


def _sb_decode_kernel(pt_ref, bias_ref, q_ref, *rest, n_pages):
    k_refs = rest[:n_pages]
    v_refs = rest[n_pages:2 * n_pages]
    o_ref = rest[2 * n_pages]
    hhd, page = k_refs[0].shape[1], k_refs[0].shape[2]
    n_heads = bias_ref.shape[0]
    hd = hhd // n_heads
    own = (lax.broadcasted_iota(jnp.int32, (n_heads, hhd), 0)
           == lax.broadcasted_iota(jnp.int32, (n_heads, hhd), 1) // hd)
    qrows = jnp.where(own, q_ref[0] * (hd ** -0.5 * LOG2E), 0.0).astype(BF16)
    rr = lax.broadcasted_iota(jnp.int32, (page, page), 0)
    ss = lax.broadcasted_iota(jnp.int32, (page, page), 1)
    u01 = jnp.where(rr > ss, 1.0, 0.0).astype(BF16)
    order = list(range(n_pages - 1, -1, -1))
    bias2 = jnp.concatenate([bias_ref[...] * LOG2E] * n_pages, axis=0)
    z_all = jnp.concatenate([_dot(qrows, k_refs[pg][0]) for pg in order], axis=0) + bias2
    sp = jnp.maximum(z_all, 0.0) + jnp.log2(1.0 + jnp.exp2(-jnp.abs(z_all)))
    tot = sp + jnp.dot(sp.astype(BF16), u01, preferred_element_type=F32)
    r_col = jnp.zeros((n_heads, 1), F32)
    ws = []
    for j in range(n_pages):
        rows = slice(j * n_heads, (j + 1) * n_heads)
        ws.append(jnp.exp2(z_all[rows] - (tot[rows] + r_col)))
        r_col = r_col + tot[rows, 0:1]
    acc = jnp.zeros((n_heads, hhd), F32)
    for w, pg in zip(ws, order):
        acc = acc + _dot_nt(w, v_refs[pg][0])
    o_ref[0] = jnp.sum(jnp.where(own, acc, 0.0), axis=0, keepdims=True)


def _sb_decode(q, cache_k, cache_v, page_table, bias):
    n, hhd = q.shape
    n_pool, page, n_heads, hd = cache_k.shape
    n_pages = page_table.shape[1]
    k_t = jnp.transpose(cache_k, (0, 2, 3, 1)).reshape(n_pool, hhd, page)
    v_t = jnp.transpose(cache_v, (0, 2, 3, 1)).reshape(n_pool, hhd, page)
    pt = page_table.reshape(-1)

    def page_spec(pg):
        return pl.BlockSpec((1, hhd, page), lambda b, pt_ref: (pt_ref[b * n_pages + pg], 0, 0))

    grid_spec = pltpu.PrefetchScalarGridSpec(
        num_scalar_prefetch=1, grid=(n,),
        in_specs=[pl.BlockSpec((n_heads, 1), lambda b, pt_ref: (0, 0)),
                  pl.BlockSpec((1, 1, hhd), lambda b, pt_ref: (b, 0, 0))]
                 + [page_spec(pg) for pg in range(n_pages)] * 2,
        out_specs=pl.BlockSpec((1, 1, hhd), lambda b, pt_ref: (b, 0, 0)))
    o = pl.pallas_call(
        functools.partial(_sb_decode_kernel, n_pages=n_pages),
        grid_spec=grid_spec, out_shape=jax.ShapeDtypeStruct((n, 1, hhd), F32),
        compiler_params=_cparams("parallel"), name="sb_decode")(
            pt, bias.reshape(n_heads, 1), q.reshape(n, 1, hhd), *([k_t] * n_pages), *([v_t] * n_pages))
    return o.reshape(n, hhd)


def _rg_decode_kernel(x_ref, y_ref, oc_ref, hist_ref, h0_ref, cw_ref, cb_ref, wa_ref, ba_ref, wx_ref, bx_ref, lam_ref,
                      o_ref, h_ref, nh_ref, *, reset):
    x = x_ref[...]
    cw = cw_ref[...]
    conv = x * cw[CONV_W - 1:CONV_W, :]
    for j in range(CONV_W - 1):
        conv = conv + hist_ref[j] * cw[j:j + 1, :]
    for j in range(CONV_W - 2):
        nh_ref[j] = hist_ref[j + 1]
    nh_ref[CONV_W - 2] = x
    xc = conv + cb_ref[...]
    xb = xc.astype(BF16)
    rg = _sigmoid(jnp.dot(xb, wa_ref[...], preferred_element_type=F32) + ba_ref[...])
    ig = _sigmoid(jnp.dot(xb, wx_ref[...], preferred_element_type=F32) + bx_ref[...])
    log_a = (-RG_C) * rg * _softplus(-lam_ref[...])
    if reset:
        h = ig * xc
    else:
        a = jnp.exp(log_a)
        h = a * h0_ref[...] + jnp.sqrt(_one_minus_exp2(log_a, a)) * (ig * xc)
    h_ref[...] = h
    d_c = oc_ref.shape[1]
    o_ref[:, 0:d_c] = oc_ref[...].astype(o_ref.dtype)
    o_ref[:, d_c:] = (h * _gelu_tanh(y_ref[...])).astype(o_ref.dtype)


def _odd_sample(oc, xr, yr, hist, h0, conv_w, conv_b, wa_bd, b_a, wx_bd, b_x, lam, offset):
    n, d = xr.shape
    hist_t = hist.transpose(1, 0, 2)
    full = lambda a: pl.BlockSpec(a.shape, lambda *_: (0,) * a.ndim)
    ins = (xr, yr, oc, hist_t, h0, conv_w, conv_b, wa_bd, b_a, wx_bd, b_x, lam)
    outs = [jax.ShapeDtypeStruct((n, oc.shape[1] + d), BF16), jax.ShapeDtypeStruct((n, d), F32),
            jax.ShapeDtypeStruct((CONV_W - 1, n, d), F32)]
    o, h, nh = pl.pallas_call(
        functools.partial(_rg_decode_kernel, reset=(offset == 0)),
        in_specs=[full(a) for a in ins], out_specs=[full(a) for a in outs], out_shape=outs,
        compiler_params=pltpu.CompilerParams(vmem_limit_bytes=VMEM_LIMIT_BYTES), name="rglru_decode")(*ins)
    return o, h, nh.transpose(1, 0, 2)


def _pad_cols(w, n):
    return jnp.pad(w, ((0, 0), (0, n - w.shape[1])))


def _block_diag(w):
    h, a, b = w.shape
    eye = jnp.eye(h, dtype=w.dtype)
    return (eye[:, None, :, None] * w[:, :, None, :]).reshape(h * a, h * b)


def _row(v, n=None):
    v = v.reshape(1, -1).astype(F32)
    return v if n is None else _pad_cols(v, n)


def kernel(x_prompt, x_sample, state_gla, state_gdn, state_gdn_conv, cache_sb_k, cache_sb_v, state_rg_h, state_rg_conv, page_table, ln_ffn, ffn_w_gate, ffn_w_up, ffn_w_down, ln_mix_e, w_in_e, gla_w_gate, gla_b_gate, gla_onorm, gdn_conv_w, gdn_a_log, gdn_dt_bias, gdn_onorm, w_out_e, ln_mix_o, w_in_o, sb_bias, rg_conv_w, rg_conv_b, rg_w_a, rg_b_a, rg_w_x, rg_b_x, rg_lambda, w_out_o, ln_final):
    bp, t, d = x_prompt.shape
    bs, ts, _ = x_sample.shape
    assert ts == 1
    depth = ln_ffn.shape[0]
    dk_a, dv_a = state_gla.shape[3], state_gla.shape[4]
    dk_b = state_gdn.shape[3]
    hd_c = cache_sb_k.shape[4]
    d_rnn = state_rg_h.shape[2]
    rank = gla_w_gate.shape[1]
    n_qa, n_va, n_b, n_c = H_A * dk_a, H_A * dv_a, H_B * dk_b, H_C * hd_c
    offset = page_table.shape[1] * cache_sb_k.shape[2]

    xp = x_prompt.reshape(bp * t, d)
    xs = x_sample.reshape(bs, d)
    outs = {}
    mix_p = mix_s = None
    wg_all, wu_all, wd_all = ffn_w_gate.astype(BF16), ffn_w_up.astype(BF16), ffn_w_down.astype(BF16)
    for layer in range(depth):
        ffn = [(_row(ln_ffn[layer, i]), wg_all, wu_all, wd_all, (layer, i)) for i in range(2)]
        if layer % 2 == 0:
            e = layer // 2
            w = w_in_e[e]
            c0 = 0
            cols = []
            for n in (n_qa, n_qa, n_va, n_va, rank, 3 * n_b, n_b, 2 * H_B):
                cols.append(w[:, c0:c0 + n])
                c0 += n
            w_q, w_k, w_v, w_r, w_la, w_qkv, w_z, w_ab = cols
            w_abla = jnp.concatenate([_pad_cols(w_ab, LANES), _pad_cols(w_la, LANES)], axis=1)
            w_list = [a.astype(BF16) for a in (w_q, w_k, w_v, w_r, w_qkv, w_z, w_abla)]
            gate = (jnp.pad(gla_w_gate[e], ((0, LANES - rank), (0, 0))).astype(BF16), _row(gla_b_gate[e]))
            proj = (_row(ln_mix_e[e]), w_list)
            w_out = w_out_e[e].astype(BF16)
        else:
            o = layer // 2
            w = w_in_o[o]
            w_list = [w[:, :n_c].astype(BF16), w[:, 3 * n_c:3 * n_c + d_rnn].astype(BF16),
                      w[:, 3 * n_c + d_rnn:].astype(BF16)]
            wt_list = [w[:, n_c:2 * n_c].T.astype(BF16), w[:, 2 * n_c:3 * n_c].T.astype(BF16)]
            gate = None
            proj = (_row(ln_mix_o[o]), w_list)
            w_out = w_out_o[o].astype(BF16)

        if layer % 2 == 0:
            xp, *pp = _block_call(xp, None, ffn[0], proj=proj, gate=gate)
            xs, *ps = _block_call(xs, None, ffn[0], proj=proj, gate=gate)
        else:
            xp, *pp = _block_call(xp, None, ffn[0], proj=proj, tproj=(wt_list, bp))
            xs, *ps = _block_call(xs, None, ffn[0], proj=proj, tproj=(wt_list, 1))

        if layer % 2 == 0:
            e = layer // 2
            r3 = lambda a: a.reshape(bp, t, a.shape[1])
            qa, ka, va, ra, qkv, zb, ab, lgp = pp
            on_a, on_b = _row(gla_onorm[e]), _row(gdn_onorm[e])
            a_log, dtb = _row(gdn_a_log[e], LANES), _row(gdn_dt_bias[e], LANES)
            oa, s_gla_p = _gla_prompt(r3(qa), r3(ka), r3(va), r3(ra), r3(lgp), on_a, bb=bp)
            ob, s_gdn_p, tail_p = _gdn_prompt(r3(qkv), r3(zb), r3(ab), gdn_conv_w[e], a_log, dtb, on_b, bb=bp)
            mix_p = ([oa.reshape(bp * t, n_va), ob.reshape(bp * t, n_b)], [w_out[:n_va], w_out[n_va:]])
            qa, ka, va, ra, qkv, zb, ab, lgp = ps
            o_s, s_gla_s, s_gdn_s, hist_s = _even_sample(qa, ka, va, ra, lgp, qkv, zb, ab, state_gla[e], state_gdn[e],
                                                         state_gdn_conv[e], gdn_conv_w[e], a_log, dtb, on_a, on_b)
            mix_s = ([o_s], [w_out])
            outs.setdefault("gla_p", []).append(s_gla_p)
            outs.setdefault("gla_s", []).append(s_gla_s)
            outs.setdefault("gdn_p", []).append(s_gdn_p)
            outs.setdefault("gdn_s", []).append(s_gdn_s)
            outs.setdefault("gcv_p", []).append(tail_p)
            outs.setdefault("gcv_s", []).append(hist_s)
        else:
            o = layer // 2
            r3 = lambda a: a.reshape(bp, t, a.shape[1])
            to_cache = lambda a: a.reshape(a.shape[0], H_C, hd_c, a.shape[2]).transpose(0, 3, 1, 2)
            qc, xr, yr, kc_t, vc_t = pp
            wa_bd, wx_bd = _block_diag(rg_w_a[o]).astype(BF16), _block_diag(rg_w_x[o]).astype(BF16)
            rg_consts = (rg_conv_w[o], _row(rg_conv_b[o]), wa_bd, _row(rg_b_a[o]), wx_bd, _row(rg_b_x[o]), _row(rg_lambda[o]))
            oc = _sb_prompt(r3(qc), kc_t, vc_t, sb_bias[o])
            od, h_p, tail_p = _rg_prompt(r3(xr), r3(yr), *rg_consts)
            mix_p = ([oc.reshape(bp * t, n_c), od.reshape(bp * t, d_rnn)], [w_out[:n_c], w_out[n_c:]])
            outs.setdefault("sbk_p", []).append(to_cache(kc_t))
            outs.setdefault("sbv_p", []).append(to_cache(vc_t))
            outs.setdefault("rgh_p", []).append(h_p)
            outs.setdefault("rgc_p", []).append(tail_p)
            qc, xr, yr, kc_t, vc_t = ps
            oc_s = _sb_decode(qc, cache_sb_k[o], cache_sb_v[o], page_table, sb_bias[o])
            o_s, h_s, hist_s = _odd_sample(oc_s, xr, yr, state_rg_conv[o], state_rg_h[o], *rg_consts, offset)
            mix_s = ([o_s], [w_out])
            outs.setdefault("sbk_s", []).append(to_cache(kc_t).reshape(bs, 1, H_C, hd_c))
            outs.setdefault("sbv_s", []).append(to_cache(vc_t).reshape(bs, 1, H_C, hd_c))
            outs.setdefault("rgh_s", []).append(h_s)
            outs.setdefault("rgc_s", []).append(hist_s)

        final_g = _row(ln_final) if layer == depth - 1 else None
        (xp,) = _block_call(xp, mix_p, ffn[1], final_g=final_g, tm=TM_POST)
        (xs,) = _block_call(xs, mix_s, ffn[1], final_g=final_g)

    st = lambda name: jnp.stack(outs[name])
    return (xp.reshape(bp, t, d), xs.reshape(bs, 1, d), st("gla_p"), st("gla_s"), st("gdn_p"), st("gdn_s"),
            st("gcv_p"), st("gcv_s"), st("sbk_p"), st("sbk_s"), st("sbv_p"), st("sbv_s"),
            st("rgh_p"), st("rgh_s"), st("rgc_p"), st("rgc_s"))
```

```python
import functools
import math

import jax
import jax.numpy as jnp
from jax import lax
from jax.experimental import pallas as pl
from jax.experimental.pallas import tpu as pltpu

F32 = jnp.float32
BF16 = jnp.bfloat16

H_A = 4
GLA_TAU = 16.0
H_B = 4
CONV_W = 4
H_C = 8
H_D = 8
RG_C = 8.0
CHUNK = 64
EPS = 1e-6
LOG2E = 1.4426950408889634

VMEM_LIMIT_BYTES = 56 * 1024 * 1024
LANES = 128
SUBLANES = 8

TM_PROJ = 256
TM_POST = 512
SB_TQ, SB_TK = 512, 256
RG_TT = 512
SAMPLE_SEQS = 8


def _cparams(*sem):
    return pltpu.CompilerParams(dimension_semantics=sem, vmem_limit_bytes=VMEM_LIMIT_BYTES)


def _dot(a, b):
    return jnp.dot(a.astype(BF16), b.astype(BF16), preferred_element_type=F32)


def _dot_nt(a, b):
    return lax.dot_general(a.astype(BF16), b.astype(BF16), (((1,), (1,)), ((), ())), preferred_element_type=F32)


def _dot_tn(a, b):
    return lax.dot_general(a.astype(BF16), b.astype(BF16), (((0,), (0,)), ((), ())), preferred_element_type=F32)


def _split2(x):
    hi = x.astype(BF16)
    lo = (x - hi.astype(F32)).astype(BF16)
    return hi, lo


def _split3(x):
    x1 = x.astype(BF16)
    r1 = x - x1.astype(F32)
    x2 = r1.astype(BF16)
    x3 = (r1 - x2.astype(F32)).astype(BF16)
    return x1, x2, x3


def _dot_hi(a, b):
    a1, a2 = a if isinstance(a, tuple) else _split2(a)
    b1, b2 = b if isinstance(b, tuple) else _split2(b)
    d = lambda u, v: jnp.dot(u, v, preferred_element_type=F32)
    return d(a1, b1) + (d(a1, b2) + d(a2, b1))


def _mask_dot(m01, x):
    x1, x2, x3 = _split3(x)
    d = lambda v: jnp.dot(m01, v, preferred_element_type=F32)
    return d(x1) + (d(x2) + d(x3))


def _rms(x, g):
    return x * lax.rsqrt(jnp.mean(x * x, axis=-1, keepdims=True) + EPS) * g


def _sigmoid(x):
    return 1.0 / (1.0 + jnp.exp(-x))


def _silu(x):
    return x * _sigmoid(x)


def _softplus(x):
    return jnp.maximum(x, 0.0) + jnp.log1p(jnp.exp(-jnp.abs(x)))


def _gelu_tanh(x):
    return 0.5 * x * (1.0 + jnp.tanh(math.sqrt(2.0 / math.pi) * (x + 0.044715 * (x * x * x))))


def _one_minus_exp2(log_a, a):
    return jnp.tanh(-log_a) * (a * a + 1.0)


def _const_spec(shape):
    nd = len(shape)
    return pl.BlockSpec(shape, lambda *_: (0,) * nd, pipeline_mode=pl.Buffered(1))


def _block_kernel(*refs, n_mix, n_proj, n_tproj, has_gate, has_final, ff_chunk):
    it = iter(refs)
    x_ref = next(it)
    o_refs = [next(it) for _ in range(n_mix)]
    wo_refs = [next(it) for _ in range(n_mix)]
    gf_ref, wg_ref, wu_ref, wd_ref = next(it), next(it), next(it), next(it)
    if has_final:
        gfin_ref = next(it)
    if n_proj:
        gm_ref = next(it)
        wp_refs = [next(it) for _ in range(n_proj)]
    if has_gate:
        wgate_ref, bgate_ref = next(it), next(it)
    wt_refs = [next(it) for _ in range(n_tproj)]
    xo_ref = next(it)
    po_refs = [next(it) for _ in range(n_proj + int(has_gate))]
    tp_refs = [next(it) for _ in range(n_tproj)]

    x = x_ref[...]
    for o_ref, wo_ref in zip(o_refs, wo_refs):
        x = x + _dot(o_ref[...], wo_ref[...])
    h = _rms(x, gf_ref[...]).astype(BF16)
    d_ff = wg_ref.shape[1]
    acc = jnp.zeros_like(x)
    for c0 in range(0, d_ff, ff_chunk):
        gt = jnp.dot(h, wg_ref[:, c0:c0 + ff_chunk], preferred_element_type=F32)
        up = jnp.dot(h, wu_ref[:, c0:c0 + ff_chunk], preferred_element_type=F32)
        act = (_silu(gt) * up).astype(BF16)
        acc = acc + jnp.dot(act, wd_ref[c0:c0 + ff_chunk, :], preferred_element_type=F32)
    x = x + 0.5 * acc
    if has_final:
        xo_ref[...] = _rms(x, gfin_ref[...])
    else:
        xo_ref[...] = x
    if n_proj:
        hm = _rms(x, gm_ref[...]).astype(BF16)
        for j, wp_ref in enumerate(wp_refs):
            p = jnp.dot(hm, wp_ref[...], preferred_element_type=F32)
            if has_gate and j == n_proj - 1:
                po_refs[j][...] = p[:, :LANES]
                po_refs[j + 1][...] = _dot(p[:, LANES:], wgate_ref[...]) + bgate_ref[...]
            else:
                po_refs[j][...] = p
        for wt_ref, tp_ref in zip(wt_refs, tp_refs):
            tp_ref[0] = lax.dot_general(wt_ref[...], hm, (((1,), (1,)), ((), ())), preferred_element_type=F32)


def _block_call(x, mix, ffn, final_g=None, proj=None, gate=None, tproj=None, tm=TM_PROJ):
    m, d = x.shape
    tm = min(tm, m)
    assert m % tm == 0
    g_ffn, wg, wu, wd, ffn_idx = ffn
    d_ff = wg.shape[-1]
    ff_chunk = d_ff
    n_lead = len(ffn_idx)

    def stacked_spec(a):
        tail = a.shape[n_lead:]
        return pl.BlockSpec((None,) * n_lead + tail, lambda *_: tuple(ffn_idx) + (0,) * len(tail),
                            pipeline_mode=pl.Buffered(1))

    row = lambda n: pl.BlockSpec((tm, n), lambda i: (i, 0))
    args, specs = [x], [row(d)]
    o_list, wo_list = mix if mix is not None else ((), ())
    for o in o_list:
        args.append(o)
        specs.append(row(o.shape[1]))
    for w in wo_list:
        args.append(w)
        specs.append(_const_spec(w.shape))
    args.append(g_ffn)
    specs.append(_const_spec(g_ffn.shape))
    for a in (wg, wu, wd):
        args.append(a)
        specs.append(stacked_spec(a))
    if final_g is not None:
        args.append(final_g)
        specs.append(_const_spec(final_g.shape))
    out_shapes, out_specs = [jax.ShapeDtypeStruct((m, d), F32)], [row(d)]
    n_proj = 0
    if proj is not None:
        g_mix, w_list = proj
        n_proj = len(w_list)
        args.append(g_mix)
        specs.append(_const_spec(g_mix.shape))
        for w in w_list:
            args.append(w)
            specs.append(_const_spec(w.shape))
        widths = [w.shape[1] for w in w_list]
        if gate is not None:
            for a in gate:
                args.append(a)
                specs.append(_const_spec(a.shape))
            assert widths[-1] == 2 * LANES
            widths[-1:] = [LANES, gate[0].shape[1]]
        for n in widths:
            out_shapes.append(jax.ShapeDtypeStruct((m, n), F32))
            out_specs.append(row(n))
    n_tproj = 0
    if tproj is not None:
        assert proj is not None
        wt_list, bsz = tproj
        n_tproj = len(wt_list)
        t_rows = m // bsz
        assert t_rows % tm == 0
        nt = t_rows // tm
        for w in wt_list:
            args.append(w)
            specs.append(_const_spec(w.shape))
            out_shapes.append(jax.ShapeDtypeStruct((bsz, w.shape[0], t_rows), F32))
            out_specs.append(pl.BlockSpec((1, w.shape[0], tm), lambda i: (i // nt, 0, i % nt)))
    kern = functools.partial(_block_kernel, n_mix=len(o_list), n_proj=n_proj, n_tproj=n_tproj,
                             has_gate=gate is not None, has_final=final_g is not None, ff_chunk=ff_chunk)
    return pl.pallas_call(
        kern, grid=(m // tm,), in_specs=specs, out_specs=out_specs, out_shape=out_shapes,
        compiler_params=_cparams("parallel"), name="block")(*args)


def _tri_incl(c):
    r = lax.broadcasted_iota(jnp.int32, (c, c), 0)
    s = lax.broadcasted_iota(jnp.int32, (c, c), 1)
    return r >= s


def _gla_kernel(q_ref, k_ref, v_ref, r_ref, lg_ref, on_ref, o_ref, st_ref, s_scr, *, bb, dk, dv):
    c_idx = pl.program_id(1)
    c = q_ref.shape[1]
    n_pairs = q_ref.shape[2] // LANES
    hpp = LANES // dk

    @pl.when(c_idx == 0)
    def _():
        s_scr[...] = jnp.zeros_like(s_scr)

    incl = _tri_incl(c)
    tri01 = jnp.where(incl, 1.0, 0.0).astype(BF16)
    lane_head = lax.broadcasted_iota(jnp.int32, (c, LANES), 1) // dk
    onorm = on_ref[...]
    gs = [_mask_dot(tri01, -_softplus(-lg_ref[b]) * (1.0 / GLA_TAU)) for b in range(bb)]
    seqs = []
    for b, g in enumerate(gs):
        g_mid = g[c // 2 - 1:c // 2, :]
        g_end = g[c - 1:c, :]
        qs = q_ref[b] * (dk ** -0.5)
        kk = k_ref[b]
        seqs.append(dict(g_end=g_end,
                         q_state=qs * jnp.exp(g),
                         q_in=qs * jnp.exp(g - g_mid),
                         k_in=(kk * jnp.exp(g_mid - g)).astype(BF16),
                         k_end=kk * jnp.exp(g_end - g)))
    probs = []
    for b, sq in enumerate(seqs):
        for p in range(n_pairs):
            sl = slice(p * LANES, (p + 1) * LANES)
            for hh in range(hpp):
                sel = lane_head == hh
                att = _dot_nt(jnp.where(sel, sq["q_in"][:, sl], 0.0), sq["k_in"][:, sl])
                probs.append(dict(b=b, p=p, hh=hh, sl=sl, sel=sel, att=jnp.where(incl, att, 0.0).astype(BF16)))
    for pr in probs:
        b, p, hh, sl = pr["b"], pr["p"], pr["hh"], pr["sl"]
        vs = slice((p * hpp + hh) * dv, (p * hpp + hh + 1) * dv)
        vh = v_ref[b, :, vs].astype(BF16)
        o = _dot(pr["att"], vh) + _dot_nt(jnp.where(pr["sel"], seqs[b]["q_state"][:, sl], 0.0), s_scr[b, p])
        o = _rms(o, onorm) * _silu(r_ref[b, :, vs])
        o_ref[b, :, vs] = o.astype(o_ref.dtype)
        pr["upd"] = _dot_tn(vh, jnp.where(pr["sel"], seqs[b]["k_end"][:, sl], 0.0))
    for b in range(bb):
        for p in range(n_pairs):
            upd = [pr["upd"] for pr in probs if pr["b"] == b and pr["p"] == p]
            sl = slice(p * LANES, (p + 1) * LANES)
            s_scr[b, p] = s_scr[b, p] * jnp.exp(seqs[b]["g_end"][:, sl]) + sum(upd[1:], upd[0])

    @pl.when(c_idx == pl.num_programs(1) - 1)
    def _():
        st_ref[...] = s_scr[...]


def _gla_prompt(q, k, v, r, lg, onorm, bb):
    bsz, t, hdk = q.shape
    hdv = v.shape[2]
    dk, dv = hdk // H_A, hdv // H_A
    c = min(CHUNK, t)
    assert t % c == 0 and bsz % bb == 0 and LANES % dk == 0
    n_pairs = hdk // LANES
    blk = lambda n: pl.BlockSpec((bb, c, n), lambda i, j: (i, j, 0))
    o, st = pl.pallas_call(
        functools.partial(_gla_kernel, bb=bb, dk=dk, dv=dv),
        grid=(bsz // bb, t // c),
        in_specs=[blk(hdk), blk(hdk), blk(hdv), blk(hdv), blk(hdk), _const_spec(onorm.shape)],
        out_specs=[blk(hdv), pl.BlockSpec((bb, n_pairs, dv, LANES), lambda i, j: (i, 0, 0, 0))],
        out_shape=[jax.ShapeDtypeStruct((bsz, t, hdv), BF16), jax.ShapeDtypeStruct((bsz, n_pairs, dv, LANES), F32)],
        scratch_shapes=[pltpu.VMEM((bb, n_pairs, dv, LANES), F32)],
        compiler_params=_cparams("parallel", "arbitrary"), name="gla_prompt")(q, k, v, r, lg, onorm)
    hpp = LANES // dk
    s = st.reshape(bsz, n_pairs, dv, hpp, dk).transpose(0, 1, 3, 4, 2).reshape(bsz, H_A, dk, dv)
    return o, s


def _gdn_kernel(x_ref, z_ref, ab_ref, cw_ref, alog_ref, dtb_ref, on_ref, o_ref, s_ref, tail_ref, xbuf, s_scr,
                *, bb, n_heads):
    c_idx = pl.program_id(1)
    c = x_ref.shape[1]
    hd = z_ref.shape[2]
    dk = hd // n_heads
    pad = SUBLANES

    @pl.when(c_idx == 0)
    def _():
        s_scr[...] = jnp.zeros_like(s_scr)
        xbuf[:, 0:pad, :] = jnp.zeros((bb, pad, xbuf.shape[2]), F32)

    cp = c * n_heads
    t_p = lax.broadcasted_iota(jnp.int32, (c, cp), 0)
    s_p = lax.broadcasted_iota(jnp.int32, (c, cp), 1) % c
    h_p = lax.broadcasted_iota(jnp.int32, (c, cp), 1) // c
    incl_p = t_p >= s_p
    strict_p = t_p > s_p
    eye_p = t_p == s_p
    one16 = lambda m: jnp.where(m, 1.0, 0.0).astype(BF16)
    blk01 = [one16(h_p == h) for h in range(n_heads)]
    tri01 = one16(_tri_incl(c))
    ones_c = jnp.ones((c, c), BF16)
    e_row = lax.broadcasted_iota(jnp.int32, (LANES, cp), 0)
    e_head = lax.broadcasted_iota(jnp.int32, (LANES, cp), 1) // c
    expand_g = one16(e_row == e_head)
    expand_b = one16(e_row == e_head + n_heads)

    def blockdiag(x16, m01=None):
        return jnp.concatenate([x16 * (blk if m01 is None else blk * m01) for blk in blk01], axis=0)

    def lane_blocks(blocks, width):
        z = jnp.zeros_like(blocks[0])
        return jnp.concatenate([jnp.concatenate([blk if i == h else z for i in range(len(blocks))], axis=1)
                                for h, blk in enumerate(blocks)], axis=0)

    def sel3(x3, m01):
        d = lambda v: jnp.dot(v, m01, preferred_element_type=F32)
        return d(x3[0]) + (d(x3[1]) + d(x3[2]))

    cw = cw_ref[...]
    onorm = on_ref[...]
    neg_a = -jnp.exp(alog_ref[...])
    dtb = dtb_ref[...]
    seqs = []
    for b in range(bb):
        x = x_ref[b]
        xbuf[b, pad:pad + c, :] = x
        conv = x * cw[CONV_W - 1:CONV_W, :]
        for j in range(1, CONV_W):
            conv = conv + xbuf[b, pad - j:pad - j + c, :] * cw[CONV_W - 1 - j:CONV_W - j, :]
        xbuf[b, 0:pad, :] = x[c - pad:c, :]
        act = _silu(conv)
        ab = ab_ref[b]
        lgb = neg_a * _softplus(ab + dtb)
        beta = _sigmoid(ab)
        g = _mask_dot(tri01, lgb)
        qs, ks, vs, gcs, bcols = [], [], [], [], []
        for h in range(n_heads):
            qh = act[:, h * dk:(h + 1) * dk]
            kh = act[:, hd + h * dk:hd + (h + 1) * dk]
            qs.append(qh * lax.rsqrt(jnp.sum(qh * qh, axis=-1, keepdims=True) + EPS) * (dk ** -0.5))
            ks.append(kh * lax.rsqrt(jnp.sum(kh * kh, axis=-1, keepdims=True) + EPS))
            vs.append(act[:, 2 * hd + h * dk:2 * hd + (h + 1) * dk])
            gcs.append(g[:, h:h + 1])
            bcols.append(beta[:, n_heads + h:n_heads + h + 1])
        gcol_p = sel3(_split3(g), expand_g)
        bcol_p = sel3(_split3(beta), expand_b)
        grow_p = _mask_dot(ones_c, jnp.where(eye_p, gcol_p, 0.0))
        rel_p = jnp.where(incl_p, jnp.exp(jnp.minimum(gcol_p - grow_p, 0.0)), 0.0)
        k16 = [kh.astype(BF16) for kh in ks]
        kbd_t = lane_blocks(k16, dk)
        kk_p = _dot_nt(jnp.concatenate(k16, axis=1), kbd_t)
        qk_p = _dot_nt(jnp.concatenate(qs, axis=1), kbd_t) * rel_p
        a_p = jnp.where(strict_p, kk_p * rel_p, 0.0) * bcol_p
        seqs.append(dict(q=qs, k=ks, v=vs, gc=gcs, bcol=bcols, eg=[jnp.exp(gc) for gc in gcs],
                         g_end=[gc[c - 1:c, :] for gc in gcs], a=a_p, qk=qk_p.astype(BF16)))

    invs = [jnp.where(eye_p, 1.0, 0.0) - jnp.where((t_p // 2) == (s_p // 2), sq["a"], 0.0) for sq in seqs]
    a_splits = [_split2(sq["a"]) for sq in seqs]
    w = 2
    while w < c:
        lvl01 = one16(((t_p // (2 * w)) == (s_p // (2 * w))) & ((t_p // w) != (s_p // w)))
        inv_splits = [_split2(inv) for inv in invs]
        xs = [_dot_hi(isp, (blockdiag(ahi, lvl01), blockdiag(alo, lvl01)))
              for isp, (ahi, alo) in zip(inv_splits, a_splits)]
        invs = [inv - _dot_hi(x, (blockdiag(isp[0]), blockdiag(isp[1])))
                for inv, x, isp in zip(invs, xs, inv_splits)]
        w *= 2
    for sq, inv in zip(seqs, invs):
        isp = _split2(inv)
        sq["w"] = _dot_hi(isp, lane_blocks([k * (bc * eg) for k, bc, eg in zip(sq["k"], sq["bcol"], sq["eg"])], dk))
        sq["u"] = _dot_hi(isp, lane_blocks([v * bc for v, bc in zip(sq["v"], sq["bcol"])], dk))

    for b, sq in enumerate(seqs):
        sts = [s_scr[b, h] for h in range(n_heads)]
        sts16 = [st.astype(BF16) for st in sts]
        v_news = [sq["u"][:, h * dk:(h + 1) * dk] - _dot(sq["w"][:, h * dk:(h + 1) * dk], sts16[h])
                  for h in range(n_heads)]
        v_stack = jnp.concatenate(v_news, axis=0).astype(BF16)
        for h in range(n_heads):
            hs = slice(h * dk, (h + 1) * dk)
            o = (jnp.dot(sq["qk"] * blk01[h], v_stack, preferred_element_type=F32)
                 + _dot(sq["q"][h] * sq["eg"][h], sts16[h]))
            s_scr[b, h] = (sts[h] * jnp.exp(sq["g_end"][h])
                           + _dot_tn(sq["k"][h] * jnp.exp(sq["g_end"][h] - sq["gc"][h]), v_news[h]))
            o = _rms(o, onorm) * _silu(z_ref[b, :, hs])
            o_ref[b, :, hs] = o.astype(o_ref.dtype)

    @pl.when(c_idx == pl.num_programs(1) - 1)
    def _():
        s_ref[...] = s_scr[...]
        tail_ref[...] = xbuf[:, 0:pad, :]


def _gdn_prompt(qkv, z, ab, conv_w, a_log, dt_bias, onorm, bb):
    bsz, t, n3 = qkv.shape
    hd = z.shape[2]
    dk = hd // H_B
    c = min(CHUNK, t)
    assert t % c == 0 and bsz % bb == 0 and c >= 8
    blk = lambda n: pl.BlockSpec((bb, c, n), lambda i, j: (i, j, 0))
    o, s, tail = pl.pallas_call(
        functools.partial(_gdn_kernel, bb=bb, n_heads=H_B),
        grid=(bsz // bb, t // c),
        in_specs=[blk(n3), blk(hd), blk(LANES), _const_spec(conv_w.shape), _const_spec(a_log.shape),
                  _const_spec(dt_bias.shape), _const_spec(onorm.shape)],
        out_specs=[blk(hd), pl.BlockSpec((bb, H_B, dk, dk), lambda i, j: (i, 0, 0, 0)),
                   pl.BlockSpec((bb, SUBLANES, n3), lambda i, j: (i, 0, 0))],
        out_shape=[jax.ShapeDtypeStruct((bsz, t, hd), BF16), jax.ShapeDtypeStruct((bsz, H_B, dk, dk), F32),
                   jax.ShapeDtypeStruct((bsz, SUBLANES, n3), F32)],
        scratch_shapes=[pltpu.VMEM((bb, SUBLANES + c, n3), F32), pltpu.VMEM((bb, H_B, dk, dk), F32)],
        compiler_params=_cparams("parallel", "arbitrary"), name="gdn_prompt")(qkv, z, ab, conv_w, a_log, dt_bias, onorm)
    return o, s, tail[:, SUBLANES - (CONV_W - 1):, :]


def _sb_tiles(qm, kblk, vblks, bias2, u01, r_cols, mask):
    z = [jnp.dot(q, kblk, preferred_element_type=F32) + b for q, b in zip(qm, bias2)]
    sp = []
    for zz in z:
        s2 = jnp.maximum(zz, 0.0) + jnp.log2(1.0 + jnp.exp2(-jnp.abs(zz)))
        sp.append(s2 if mask is None else jnp.where(mask, s2, 0.0))
    inner = [jnp.dot(s2.astype(BF16), u01, preferred_element_type=F32) for s2 in sp]
    pv = None
    r_new = []
    for hh, (zz, s2, inn, r) in enumerate(zip(z, sp, inner, r_cols)):
        tot = s2 + inn
        w = jnp.exp2(zz - (tot + r))
        if mask is not None:
            w = jnp.where(mask, w, 0.0)
        r_new.append(r + tot[:, 0:1])
        d = _dot_nt(w, vblks[hh])
        pv = d if pv is None else pv + d
    return pv, r_new


def _sb_kernel(bias_ref, q_ref, k_ref, v_ref, o_ref, kb, vm, acc, rsum, *, hd, tk):
    p = pl.program_id(1)
    qi = pl.program_id(2)
    tq = q_ref.shape[1]
    t = k_ref.shape[2]
    hpp = LANES // hd
    row_head = lax.broadcasted_iota(jnp.int32, (LANES, tk), 0) // hd

    @pl.when(qi == 0)
    def _():
        for jj in range(t // tk):
            kb[jj] = k_ref[0, :, jj * tk:(jj + 1) * tk].astype(BF16)
            v = v_ref[0, :, jj * tk:(jj + 1) * tk]
            for hh in range(hpp):
                vm[hh, jj] = jnp.where(row_head == hh, v, 0.0).astype(BF16)

    lane_head_q = lax.broadcasted_iota(jnp.int32, (tq, LANES), 1) // hd
    qs = q_ref[0] * (hd ** -0.5 * LOG2E)
    qm = [jnp.where(lane_head_q == hh, qs, 0.0).astype(BF16) for hh in range(hpp)]
    bias2 = [bias_ref[p * hpp + hh] * LOG2E for hh in range(hpp)]
    rr = lax.broadcasted_iota(jnp.int32, (tk, tk), 0)
    ss = lax.broadcasted_iota(jnp.int32, (tk, tk), 1)
    u01 = jnp.where(rr > ss, 1.0, 0.0).astype(BF16)
    n_sub = tq // tk

    acc[...] = jnp.zeros_like(acc)
    rsum[...] = jnp.zeros_like(rsum)

    def tile(jj, mask, rows=slice(None)):
        vblks = [vm[hh, jj] for hh in range(hpp)]
        pv, r_new = _sb_tiles([q[rows] for q in qm], kb[jj], vblks, bias2, u01,
                              [rsum[hh, rows] for hh in range(hpp)], mask)
        acc[rows] += pv
        for hh in range(hpp):
            rsum[hh, rows] = r_new[hh]

    strict_lower = lax.broadcasted_iota(jnp.int32, (tk, tk), 1) < lax.broadcasted_iota(jnp.int32, (tk, tk), 0)
    for rb in range(n_sub):
        rows = slice(rb * tk, (rb + 1) * tk)
        tile(qi * n_sub + rb, strict_lower, rows)
        for sub in range(rb - 1, -1, -1):
            tile(qi * n_sub + sub, None, rows)

    n_vis = qi * n_sub
    unroll = 4 * n_sub
    rem = n_vis % unroll
    done = 0
    run = n_sub
    while run < unroll:
        take = (rem // run) % 2 == 1

        @pl.when(take)
        def _(done=done, run=run):
            for sub in range(run):
                tile(n_vis - 1 - done - sub, None)

        done = done + jnp.where(take, run, 0)
        run *= 2

    def body(i, carry):
        for sub in range(unroll):
            tile(n_vis - rem - 1 - i * unroll - sub, None)
        return carry

    lax.fori_loop(0, n_vis // unroll, body, 0)
    o_ref[0] = acc[...].astype(o_ref.dtype)


def _sb_prompt(q, k_t, v_t, bias, tq=SB_TQ, tk=SB_TK):
    bsz, t, hhd = q.shape
    hd = hhd // H_C
    tq, tk = min(tq, t), min(tk, t)
    assert t % tq == 0 and tq % tk == 0 and LANES % hd == 0
    n_pairs = hhd // LANES
    hpp = LANES // hd
    return pl.pallas_call(
        functools.partial(_sb_kernel, hd=hd, tk=tk),
        grid=(bsz, n_pairs, t // tq),
        in_specs=[pl.BlockSpec(memory_space=pltpu.SMEM),
                  pl.BlockSpec((1, tq, LANES), lambda b, p, i: (b, i, p)),
                  pl.BlockSpec((1, LANES, t), lambda b, p, i: (b, p, 0)),
                  pl.BlockSpec((1, LANES, t), lambda b, p, i: (b, p, 0))],
        out_specs=pl.BlockSpec((1, tq, LANES), lambda b, p, i: (b, i, p)),
        out_shape=jax.ShapeDtypeStruct((bsz, t, hhd), BF16),
        scratch_shapes=[pltpu.VMEM((t // tk, LANES, tk), BF16), pltpu.VMEM((hpp, t // tk, LANES, tk), BF16),
                        pltpu.VMEM((tq, LANES), F32), pltpu.VMEM((hpp, tq, 1), F32)],
        compiler_params=_cparams("parallel", "parallel", "arbitrary"), name="sb_prompt")(bias, q, k_t, v_t)


def _rg_kernel(x_ref, y_ref, cw_ref, cb_ref, wa_ref, ba_ref, wx_ref, bx_ref, lam_ref,
               o_ref, h_ref, tail_ref, xbuf, a_scr, b_scr, h_scr):
    ti = pl.program_id(1)
    tt = x_ref.shape[1]
    d = x_ref.shape[2]
    pad = SUBLANES

    @pl.when(ti == 0)
    def _():
        h_scr[...] = jnp.zeros_like(h_scr)
        xbuf[0:pad, :] = jnp.zeros((pad, d), F32)

    x = x_ref[0]
    cw = cw_ref[...]
    xbuf[pad:pad + tt, :] = x
    conv = x * cw[CONV_W - 1:CONV_W, :]
    for j in range(1, CONV_W):
        conv = conv + xbuf[pad - j:pad - j + tt, :] * cw[CONV_W - 1 - j:CONV_W - j, :]
    xbuf[0:pad, :] = x[tt - pad:tt, :]
    xc = conv + cb_ref[...]
    nsp = _softplus(-lam_ref[...])
    xb = xc.astype(BF16)
    rg = _sigmoid(jnp.dot(xb, wa_ref[...], preferred_element_type=F32) + ba_ref[...])
    ig = _sigmoid(jnp.dot(xb, wx_ref[...], preferred_element_type=F32) + bx_ref[...])
    log_a = (-RG_C) * rg * nsp
    a = jnp.exp(log_a)
    bt = jnp.sqrt(_one_minus_exp2(log_a, a)) * (ig * xc)
    pos = ti * tt + lax.broadcasted_iota(jnp.int32, (tt, d), 0)
    a = jnp.where(pos == 0, 0.0, a)
    bt = jnp.where(pos == 0, ig * xc, bt)

    grp = SUBLANES
    row_in_grp = lax.broadcasted_iota(jnp.int32, (tt, d), 0) % grp
    sh = 1
    while sh < grp:
        a_prev = jnp.where(row_in_grp >= sh, pltpu.roll(a, sh, 0), 1.0)
        b_prev = jnp.where(row_in_grp >= sh, pltpu.roll(bt, sh, 0), 0.0)
        bt = a * b_prev + bt
        a = a * a_prev
        sh *= 2
    a_scr[...] = a
    b_scr[...] = bt

    def body(i, h):
        r0 = pl.multiple_of(i * grp, grp)
        hg = a_scr[pl.ds(r0, grp), :] * h + b_scr[pl.ds(r0, grp), :]
        b_scr[pl.ds(r0, grp), :] = hg
        return hg[grp - 1:grp, :]

    h_last = lax.fori_loop(0, tt // grp, body, h_scr[...])
    h_scr[...] = h_last
    hs = b_scr[...]
    o_ref[0] = (hs * _gelu_tanh(y_ref[0])).astype(o_ref.dtype)

    @pl.when(ti == pl.num_programs(1) - 1)
    def _():
        h_ref[0] = h_last
        tail_ref[0] = xbuf[0:pad, :]


def _rg_prompt(xr, yr, conv_w, conv_b, wa_bd, b_a, wx_bd, b_x, lam, tt=RG_TT):
    bsz, t, d = xr.shape
    tt = min(tt, t)
    assert t % tt == 0 and tt % SUBLANES == 0
    blk = pl.BlockSpec((1, tt, d), lambda b, i: (b, i, 0))
    consts = (conv_w, conv_b, wa_bd, b_a, wx_bd, b_x, lam)
    o, h, tail = pl.pallas_call(
        _rg_kernel,
        grid=(bsz, t // tt),
        in_specs=[blk, blk] + [_const_spec(a.shape) for a in consts],
        out_specs=[blk, pl.BlockSpec((1, 1, d), lambda b, i: (b, 0, 0)), pl.BlockSpec((1, SUBLANES, d), lambda b, i: (b, 0, 0))],
        out_shape=[jax.ShapeDtypeStruct((bsz, t, d), BF16), jax.ShapeDtypeStruct((bsz, 1, d), F32),
                   jax.ShapeDtypeStruct((bsz, SUBLANES, d), F32)],
        scratch_shapes=[pltpu.VMEM((SUBLANES + tt, d), F32), pltpu.VMEM((tt, d), F32), pltpu.VMEM((tt, d), F32),
                        pltpu.VMEM((1, d), F32)],
        compiler_params=_cparams("parallel", "arbitrary"), name="rglru_prompt")(xr, yr, *consts)
    return o, h[:, 0, :], tail[:, SUBLANES - (CONV_W - 1):, :]


def _even_prep_kernel(x_ref, hist_ref, cw_ref, ab_ref, alog_ref, dtb_ref, lgp_ref,
                      q_ref, k_ref, v_ref, nh_ref, gb_ref, lg_ref, *, n_heads):
    x = x_ref[...]
    cw = cw_ref[...]
    conv = x * cw[CONV_W - 1:CONV_W, :]
    for j in range(CONV_W - 1):
        conv = conv + hist_ref[j] * cw[j:j + 1, :]
    for j in range(CONV_W - 2):
        nh_ref[j] = hist_ref[j + 1]
    nh_ref[CONV_W - 2] = x
    act = _silu(conv)
    hd = q_ref.shape[1]
    dk = hd // n_heads
    for h in range(n_heads):
        hs = slice(h * dk, (h + 1) * dk)
        qh = act[:, hs]
        kh = act[:, hd + h * dk:hd + (h + 1) * dk]
        q_ref[:, hs] = qh * lax.rsqrt(jnp.sum(qh * qh, axis=-1, keepdims=True) + EPS) * (dk ** -0.5)
        k_ref[:, hs] = kh * lax.rsqrt(jnp.sum(kh * kh, axis=-1, keepdims=True) + EPS)
    v_ref[...] = act[:, 2 * hd:3 * hd]
    ab = ab_ref[...]
    lane = lax.broadcasted_iota(jnp.int32, ab.shape, 1)
    lgb = -jnp.exp(alog_ref[...]) * _softplus(ab + dtb_ref[...])
    gb_ref[...] = jnp.where(lane < n_heads, lgb, _sigmoid(ab))
    lg_ref[...] = -_softplus(-lgp_ref[...]) * (1.0 / GLA_TAU)


def _even_state_kernel(qa_ref, ka_ref, lga_ref, va_ref, qb_ref, kb_ref, vb_ref, gb_ref, sa_ref, sb_ref,
                       oa_ref, ob_ref, sa_out, sb_out, *, nb, dk_a, dv_a, dk_b):
    for j in range(nb):
        gbrow = gb_ref[j:j + 1, :]
        for h in range(H_A):
            rs = slice(h * dk_a, (h + 1) * dk_a)
            s_new = sa_ref[j, h] * jnp.exp(lga_ref[0, rs, j:j + 1]) + ka_ref[0, rs, j:j + 1] * va_ref[j:j + 1, h * dv_a:(h + 1) * dv_a]
            sa_out[j, h] = s_new
            oa_ref[j:j + 1, h * dv_a:(h + 1) * dv_a] = jnp.sum(qa_ref[0, rs, j:j + 1] * (dk_a ** -0.5) * s_new, axis=0, keepdims=True)
        for h in range(H_B):
            rs = slice(h * dk_b, (h + 1) * dk_b)
            st = sb_ref[j, h]
            eg = jnp.exp(gbrow[:, h:h + 1])
            beta = gbrow[:, H_B + h:H_B + h + 1]
            kcol = kb_ref[0, rs, j:j + 1]
            vrow = vb_ref[j:j + 1, rs]
            v_new = beta * (vrow - eg * jnp.sum(kcol * st, axis=0, keepdims=True))
            s_new = st * eg + kcol * v_new
            sb_out[j, h] = s_new
            ob_ref[j:j + 1, rs] = jnp.sum(qb_ref[0, rs, j:j + 1] * s_new, axis=0, keepdims=True)


def _even_out_kernel(oa_ref, ob_ref, r_ref, z_ref, ona_ref, onb_ref, o_ref, *, dv_a, dv_b):
    n_a = oa_ref.shape[1]
    for h in range(n_a // dv_a):
        sl = slice(h * dv_a, (h + 1) * dv_a)
        o_ref[:, sl] = (_rms(oa_ref[:, sl], ona_ref[...]) * _silu(r_ref[:, sl])).astype(o_ref.dtype)
    for h in range(ob_ref.shape[1] // dv_b):
        sl = slice(h * dv_b, (h + 1) * dv_b)
        o_ref[:, n_a + h * dv_b:n_a + (h + 1) * dv_b] = (_rms(ob_ref[:, sl], onb_ref[...]) * _silu(z_ref[:, sl])).astype(o_ref.dtype)


def _cols(x, nb):
    n, d = x.shape
    return x.reshape(n // nb, nb, d).transpose(0, 2, 1)


def _even_sample(qa, ka, va, ra, lgp, qkv, zb, ab, s_gla, s_gdn, hist, conv_w, a_log, dt_bias, on_a, on_b, nb=SAMPLE_SEQS):
    n, hdk_a = qa.shape
    hd_b = zb.shape[1]
    n3 = qkv.shape[1]
    hist_t = hist.transpose(1, 0, 2)
    full = lambda a: pl.BlockSpec(a.shape, lambda *_: (0,) * a.ndim)
    sds = lambda *s: jax.ShapeDtypeStruct(s, F32)
    ins = (qkv, hist_t, conv_w, ab, a_log, dt_bias, lgp)
    outs = [sds(n, hd_b), sds(n, hd_b), sds(n, hd_b), sds(CONV_W - 1, n, n3), sds(n, LANES), sds(n, hdk_a)]
    qb, kb, vb, new_hist_t, gb, lga = pl.pallas_call(
        functools.partial(_even_prep_kernel, n_heads=H_B),
        in_specs=[full(a) for a in ins], out_specs=[full(o) for o in outs], out_shape=outs,
        compiler_params=pltpu.CompilerParams(vmem_limit_bytes=VMEM_LIMIT_BYTES), name="even_sample_prep")(*ins)

    dk_a, dv_a, dk_b = hdk_a // H_A, va.shape[1] // H_A, hd_b // H_B
    assert n % nb == 0
    colspec = lambda d: pl.BlockSpec((1, d, nb), lambda i: (i, 0, 0))
    rowspec = lambda d: pl.BlockSpec((nb, d), lambda i: (i, 0))
    sa_spec = pl.BlockSpec((nb, H_A, dk_a, dv_a), lambda i: (i, 0, 0, 0))
    sb_spec = pl.BlockSpec((nb, H_B, dk_b, dk_b), lambda i: (i, 0, 0, 0))
    oa, ob, s_gla_new, s_gdn_new = pl.pallas_call(
        functools.partial(_even_state_kernel, nb=nb, dk_a=dk_a, dv_a=dv_a, dk_b=dk_b),
        grid=(n // nb,),
        in_specs=[colspec(hdk_a), colspec(hdk_a), colspec(hdk_a), rowspec(va.shape[1]),
                  colspec(hd_b), colspec(hd_b), rowspec(hd_b), rowspec(LANES), sa_spec, sb_spec],
        out_specs=[rowspec(va.shape[1]), rowspec(hd_b), sa_spec, sb_spec],
        out_shape=[sds(n, va.shape[1]), sds(n, hd_b), sds(*s_gla.shape), sds(*s_gdn.shape)],
        compiler_params=_cparams("parallel"), name="even_sample_state")(
            _cols(qa, nb), _cols(ka, nb), _cols(lga, nb), va, _cols(qb, nb), _cols(kb, nb), vb, gb, s_gla, s_gdn)

    ins = (oa, ob, ra, zb, on_a, on_b)
    out = jax.ShapeDtypeStruct((n, va.shape[1] + hd_b), BF16)
    o = pl.pallas_call(
        functools.partial(_even_out_kernel, dv_a=dv_a, dv_b=dk_b),
        in_specs=[full(a) for a in ins], out_specs=full(out), out_shape=out,
        compiler_params=pltpu.CompilerParams(vmem_limit_bytes=VMEM_LIMIT_BYTES), name="even_sample_out")(*ins)
    return o, s_gla_new, s_gdn_new, new_hist_t.transpose(1, 0, 2)


def _sb_decode_kernel(pt_ref, bias_ref, q_ref, *rest, n_pages):
    k_refs = rest[:n_pages]
    v_refs = rest[n_pages:2 * n_pages]
    o_ref = rest[2 * n_pages]
    hhd, page = k_refs[0].shape[1], k_refs[0].shape[2]
    n_heads = bias_ref.shape[0]
    hd = hhd // n_heads
    own = (lax.broadcasted_iota(jnp.int32, (n_heads, hhd), 0)
           == lax.broadcasted_iota(jnp.int32, (n_heads, hhd), 1) // hd)
    qrows = jnp.where(own, q_ref[0] * (hd ** -0.5 * LOG2E), 0.0).astype(BF16)
    rr = lax.broadcasted_iota(jnp.int32, (page, page), 0)
    ss = lax.broadcasted_iota(jnp.int32, (page, page), 1)
    u01 = jnp.where(rr > ss, 1.0, 0.0).astype(BF16)
    order = list(range(n_pages - 1, -1, -1))
    bias2 = jnp.concatenate([bias_ref[...] * LOG2E] * n_pages, axis=0)
    z_all = jnp.concatenate([_dot(qrows, k_refs[pg][0]) for pg in order], axis=0) + bias2
    sp = jnp.maximum(z_all, 0.0) + jnp.log2(1.0 + jnp.exp2(-jnp.abs(z_all)))
    tot = sp + jnp.dot(sp.astype(BF16), u01, preferred_element_type=F32)
    r_col = jnp.zeros((n_heads, 1), F32)
    ws = []
    for j in range(n_pages):
        rows = slice(j * n_heads, (j + 1) * n_heads)
        ws.append(jnp.exp2(z_all[rows] - (tot[rows] + r_col)))
        r_col = r_col + tot[rows, 0:1]
    acc = jnp.zeros((n_heads, hhd), F32)
    for w, pg in zip(ws, order):
        acc = acc + _dot_nt(w, v_refs[pg][0])
    o_ref[0] = jnp.sum(jnp.where(own, acc, 0.0), axis=0, keepdims=True)


def _sb_decode(q, cache_k, cache_v, page_table, bias):
    n, hhd = q.shape
    n_pool, page, n_heads, hd = cache_k.shape
    n_pages = page_table.shape[1]
    k_t = jnp.transpose(cache_k, (0, 2, 3, 1)).reshape(n_pool, hhd, page)
    v_t = jnp.transpose(cache_v, (0, 2, 3, 1)).reshape(n_pool, hhd, page)
    pt = page_table.reshape(-1)

    def page_spec(pg):
        return pl.BlockSpec((1, hhd, page), lambda b, pt_ref: (pt_ref[b * n_pages + pg], 0, 0))

    grid_spec = pltpu.PrefetchScalarGridSpec(
        num_scalar_prefetch=1, grid=(n,),
        in_specs=[pl.BlockSpec((n_heads, 1), lambda b, pt_ref: (0, 0)),
                  pl.BlockSpec((1, 1, hhd), lambda b, pt_ref: (b, 0, 0))]
                 + [page_spec(pg) for pg in range(n_pages)] * 2,
        out_specs=pl.BlockSpec((1, 1, hhd), lambda b, pt_ref: (b, 0, 0)))
    o = pl.pallas_call(
        functools.partial(_sb_decode_kernel, n_pages=n_pages),
        grid_spec=grid_spec, out_shape=jax.ShapeDtypeStruct((n, 1, hhd), F32),
        compiler_params=_cparams("parallel"), name="sb_decode")(
            pt, bias.reshape(n_heads, 1), q.reshape(n, 1, hhd), *([k_t] * n_pages), *([v_t] * n_pages))
    return o.reshape(n, hhd)


def _rg_decode_kernel(x_ref, y_ref, oc_ref, hist_ref, h0_ref, cw_ref, cb_ref, wa_ref, ba_ref, wx_ref, bx_ref, lam_ref,
                      o_ref, h_ref, nh_ref, *, reset):
    x = x_ref[...]
    cw = cw_ref[...]
    conv = x * cw[CONV_W - 1:CONV_W, :]
    for j in range(CONV_W - 1):
        conv = conv + hist_ref[j] * cw[j:j + 1, :]
    for j in range(CONV_W - 2):
        nh_ref[j] = hist_ref[j + 1]
    nh_ref[CONV_W - 2] = x
    xc = conv + cb_ref[...]
    xb = xc.astype(BF16)
    rg = _sigmoid(jnp.dot(xb, wa_ref[...], preferred_element_type=F32) + ba_ref[...])
    ig = _sigmoid(jnp.dot(xb, wx_ref[...], preferred_element_type=F32) + bx_ref[...])
    log_a = (-RG_C) * rg * _softplus(-lam_ref[...])
    if reset:
        h = ig * xc
    else:
        a = jnp.exp(log_a)
        h = a * h0_ref[...] + jnp.sqrt(_one_minus_exp2(log_a, a)) * (ig * xc)
    h_ref[...] = h
    d_c = oc_ref.shape[1]
    o_ref[:, 0:d_c] = oc_ref[...].astype(o_ref.dtype)
    o_ref[:, d_c:] = (h * _gelu_tanh(y_ref[...])).astype(o_ref.dtype)


def _odd_sample(oc, xr, yr, hist, h0, conv_w, conv_b, wa_bd, b_a, wx_bd, b_x, lam, offset):
    n, d = xr.shape
    hist_t = hist.transpose(1, 0, 2)
    full = lambda a: pl.BlockSpec(a.shape, lambda *_: (0,) * a.ndim)
    ins = (xr, yr, oc, hist_t, h0, conv_w, conv_b, wa_bd, b_a, wx_bd, b_x, lam)
    outs = [jax.ShapeDtypeStruct((n, oc.shape[1] + d), BF16), jax.ShapeDtypeStruct((n, d), F32),
            jax.ShapeDtypeStruct((CONV_W - 1, n, d), F32)]
    o, h, nh = pl.pallas_call(
        functools.partial(_rg_decode_kernel, reset=(offset == 0)),
        in_specs=[full(a) for a in ins], out_specs=[full(a) for a in outs], out_shape=outs,
        compiler_params=pltpu.CompilerParams(vmem_limit_bytes=VMEM_LIMIT_BYTES), name="rglru_decode")(*ins)
    return o, h, nh.transpose(1, 0, 2)


def _pad_cols(w, n):
    return jnp.pad(w, ((0, 0), (0, n - w.shape[1])))


def _block_diag(w):
    h, a, b = w.shape
    eye = jnp.eye(h, dtype=w.dtype)
    return (eye[:, None, :, None] * w[:, :, None, :]).reshape(h * a, h * b)


def _row(v, n=None):
    v = v.reshape(1, -1).astype(F32)
    return v if n is None else _pad_cols(v, n)


def kernel(x_prompt, x_sample, state_gla, state_gdn, state_gdn_conv, cache_sb_k, cache_sb_v, state_rg_h, state_rg_conv, page_table, ln_ffn, ffn_w_gate, ffn_w_up, ffn_w_down, ln_mix_e, w_in_e, gla_w_gate, gla_b_gate, gla_onorm, gdn_conv_w, gdn_a_log, gdn_dt_bias, gdn_onorm, w_out_e, ln_mix_o, w_in_o, sb_bias, rg_conv_w, rg_conv_b, rg_w_a, rg_b_a, rg_w_x, rg_b_x, rg_lambda, w_out_o, ln_final):
    bp, t, d = x_prompt.shape
    bs, ts, _ = x_sample.shape
    assert ts == 1
    depth = ln_ffn.shape[0]
    dk_a, dv_a = state_gla.shape[3], state_gla.shape[4]
    dk_b = state_gdn.shape[3]
    hd_c = cache_sb_k.shape[4]
    d_rnn = state_rg_h.shape[2]
    rank = gla_w_gate.shape[1]
    n_qa, n_va, n_b, n_c = H_A * dk_a, H_A * dv_a, H_B * dk_b, H_C * hd_c
    offset = page_table.shape[1] * cache_sb_k.shape[2]

    xp = x_prompt.reshape(bp * t, d)
    xs = x_sample.reshape(bs, d)
    outs = {}
    mix_p = mix_s = None
    wg_all, wu_all, wd_all = ffn_w_gate.astype(BF16), ffn_w_up.astype(BF16), ffn_w_down.astype(BF16)
    for layer in range(depth):
        ffn = [(_row(ln_ffn[layer, i]), wg_all, wu_all, wd_all, (layer, i)) for i in range(2)]
        if layer % 2 == 0:
            e = layer // 2
            w = w_in_e[e]
            c0 = 0
            cols = []
            for n in (n_qa, n_qa, n_va, n_va, rank, 3 * n_b, n_b, 2 * H_B):
                cols.append(w[:, c0:c0 + n])
                c0 += n
            w_q, w_k, w_v, w_r, w_la, w_qkv, w_z, w_ab = cols
            w_abla = jnp.concatenate([_pad_cols(w_ab, LANES), _pad_cols(w_la, LANES)], axis=1)
            w_list = [a.astype(BF16) for a in (w_q, w_k, w_v, w_r, w_qkv, w_z, w_abla)]
            gate = (jnp.pad(gla_w_gate[e], ((0, LANES - rank), (0, 0))).astype(BF16), _row(gla_b_gate[e]))
            proj = (_row(ln_mix_e[e]), w_list)
            w_out = w_out_e[e].astype(BF16)
        else:
            o = layer // 2
            w = w_in_o[o]
            w_list = [w[:, :n_c].astype(BF16), w[:, 3 * n_c:3 * n_c + d_rnn].astype(BF16),
                      w[:, 3 * n_c + d_rnn:].astype(BF16)]
            wt_list = [w[:, n_c:2 * n_c].T.astype(BF16), w[:, 2 * n_c:3 * n_c].T.astype(BF16)]
            gate = None
            proj = (_row(ln_mix_o[o]), w_list)
            w_out = w_out_o[o].astype(BF16)

        if layer % 2 == 0:
            xp, *pp = _block_call(xp, None, ffn[0], proj=proj, gate=gate)
            xs, *ps = _block_call(xs, None, ffn[0], proj=proj, gate=gate)
        else:
            xp, *pp = _block_call(xp, None, ffn[0], proj=proj, tproj=(wt_list, bp))
            xs, *ps = _block_call(xs, None, ffn[0], proj=proj, tproj=(wt_list, 1))

        if layer % 2 == 0:
            e = layer // 2
            r3 = lambda a: a.reshape(bp, t, a.shape[1])
            qa, ka, va, ra, qkv, zb, ab, lgp = pp
            on_a, on_b = _row(gla_onorm[e]), _row(gdn_onorm[e])
            a_log, dtb = _row(gdn_a_log[e], LANES), _row(gdn_dt_bias[e], LANES)
            oa, s_gla_p = _gla_prompt(r3(qa), r3(ka), r3(va), r3(ra), r3(lgp), on_a, bb=bp)
            ob, s_gdn_p, tail_p = _gdn_prompt(r3(qkv), r3(zb), r3(ab), gdn_conv_w[e], a_log, dtb, on_b, bb=bp)
            mix_p = ([oa.reshape(bp * t, n_va), ob.reshape(bp * t, n_b)], [w_out[:n_va], w_out[n_va:]])
            qa, ka, va, ra, qkv, zb, ab, lgp = ps
            o_s, s_gla_s, s_gdn_s, hist_s = _even_sample(qa, ka, va, ra, lgp, qkv, zb, ab, state_gla[e], state_gdn[e],
                                                         state_gdn_conv[e], gdn_conv_w[e], a_log, dtb, on_a, on_b)
            mix_s = ([o_s], [w_out])
            outs.setdefault("gla_p", []).append(s_gla_p)
            outs.setdefault("gla_s", []).append(s_gla_s)
            outs.setdefault("gdn_p", []).append(s_gdn_p)
            outs.setdefault("gdn_s", []).append(s_gdn_s)
            outs.setdefault("gcv_p", []).append(tail_p)
            outs.setdefault("gcv_s", []).append(hist_s)
        else:
            o = layer // 2
            r3 = lambda a: a.reshape(bp, t, a.shape[1])
            to_cache = lambda a: a.reshape(a.shape[0], H_C, hd_c, a.shape[2]).transpose(0, 3, 1, 2)
            qc, xr, yr, kc_t, vc_t = pp
            wa_bd, wx_bd = _block_diag(rg_w_a[o]).astype(BF16), _block_diag(rg_w_x[o]).astype(BF16)
            rg_consts = (rg_conv_w[o], _row(rg_conv_b[o]), wa_bd, _row(rg_b_a[o]), wx_bd, _row(rg_b_x[o]), _row(rg_lambda[o]))
            oc = _sb_prompt(r3(qc), kc_t, vc_t, sb_bias[o])
            od, h_p, tail_p = _rg_prompt(r3(xr), r3(yr), *rg_consts)
            mix_p = ([oc.reshape(bp * t, n_c), od.reshape(bp * t, d_rnn)], [w_out[:n_c], w_out[n_c:]])
            outs.setdefault("sbk_p", []).append(to_cache(kc_t))
            outs.setdefault("sbv_p", []).append(to_cache(vc_t))
            outs.setdefault("rgh_p", []).append(h_p)
            outs.setdefault("rgc_p", []).append(tail_p)
            qc, xr, yr, kc_t, vc_t = ps
            oc_s = _sb_decode(qc, cache_sb_k[o], cache_sb_v[o], page_table, sb_bias[o])
            o_s, h_s, hist_s = _odd_sample(oc_s, xr, yr, state_rg_conv[o], state_rg_h[o], *rg_consts, offset)
            mix_s = ([o_s], [w_out])
            outs.setdefault("sbk_s", []).append(to_cache(kc_t).reshape(bs, 1, H_C, hd_c))
            outs.setdefault("sbv_s", []).append(to_cache(vc_t).reshape(bs, 1, H_C, hd_c))
            outs.setdefault("rgh_s", []).append(h_s)
            outs.setdefault("rgc_s", []).append(hist_s)

        final_g = _row(ln_final) if layer == depth - 1 else None
        (xp,) = _block_call(xp, mix_p, ffn[1], final_g=final_g, tm=TM_POST)
        (xs,) = _block_call(xs, mix_s, ffn[1], final_g=final_g)

    st = lambda name: jnp.stack(outs[name])
    return (xp.reshape(bp, t, d), xs.reshape(bs, 1, d), st("gla_p"), st("gla_s"), st("gdn_p"), st("gdn_s"),
            st("gcv_p"), st("gcv_s"), st("sbk_p"), st("sbk_s"), st("sbv_p"), st("sbv_s"),
            st("rgh_p"), st("rgh_s"), st("rgc_p"), st("rgc_s"))
```
